```python
import jax
import jax.numpy as jnp
from jax import lax
import numpy as np

D_MODEL = 1024
BATCH = 8
SEQ = 4096
DEPTH = 4
DEC_BATCH = 32
DEC_SEQ = 64
PAST_LEN = 1024

CHUNK = 64
ML_HEADS = 4
ML_DH = 128
ML_W = ML_HEADS * ML_DH
SG_GROUPS = 4
SG_CHUNK = 128
SG_W = 512
SG_GDIM = SG_W // SG_GROUPS
SW_HEADS = 8
SW_KV = 2
SW_REP = SW_HEADS // SW_KV
SW_DH = 64
SW_W = SW_HEADS * SW_DH
SW_KVW = SW_KV * SW_DH
WINDOW = 128
WIN_CHUNKS = WINDOW // CHUNK
BAND = (WIN_CHUNKS + 1) * CHUNK
ROPE_THETA = 10000.0
FFN_HIDDEN = 2816
N_BRANCH = 3
EPS = 1e-6
SPLIT_SIZES = (ML_W, ML_W, ML_W, ML_W, ML_HEADS, ML_HEADS, SG_W, SG_W, SW_W, SW_KVW, SW_KVW, N_BRANCH * D_MODEL)
D_IN = 4 * ML_W + 2 * ML_HEADS + 2 * SG_W + SW_W + 2 * SW_KVW + N_BRANCH * D_MODEL

kernel_name = 'hybrid_mlstm_gmlp_swa_stream_step'


def rms_norm(x, g):
    xf = x.astype(jnp.float32)
    y = xf * lax.rsqrt(jnp.mean(xf * xf, axis=-1, keepdims=True) + EPS)
    return (y * g.astype(jnp.float32)).astype(x.dtype)


def layer_norm(x, g, b):
    xf = x.astype(jnp.float32)
    mu = jnp.mean(xf, axis=-1, keepdims=True)
    var = jnp.mean(jnp.square(xf - mu), axis=-1, keepdims=True)
    y = (xf - mu) * lax.rsqrt(var + EPS)
    return (y * g.astype(jnp.float32) + b.astype(jnp.float32)).astype(x.dtype)


def rope(x, pos):
    half = x.shape[-1] // 2
    inv = ROPE_THETA ** (-jnp.arange(half, dtype=jnp.float32) / half)
    ang = pos.astype(jnp.float32)[:, None] * inv[None, :]
    cos = jnp.cos(ang)[:, None, :]
    sin = jnp.sin(ang)[:, None, :]
    xf = x.astype(jnp.float32)
    x1, x2 = xf[..., :half], xf[..., half:]
    return jnp.concatenate([x1 * cos - x2 * sin, x1 * sin + x2 * cos], axis=-1).astype(x.dtype)


def split_in(z):
    idx = np.cumsum(np.array(SPLIT_SIZES))[:-1].tolist()
    return jnp.split(z, idx, axis=-1)


def mlstm_chunk(carry, q, k, v, ig, lf):
    C, n, m = carry
    L = q.shape[2]
    b = jnp.cumsum(lf, axis=-1)
    logD = b[..., :, None] - b[..., None, :] + ig[..., None, :]
    logD = jnp.where(jnp.tril(jnp.ones((L, L), bool)), logD, -jnp.inf)
    inter = b + m[..., None]
    m_t = jnp.maximum(inter, jnp.max(logD, axis=-1))
    w_intra = jnp.exp(logD - m_t[..., None])
    w_inter = jnp.exp(inter - m_t)
    s = jnp.einsum('bhtd,bhsd->bhts', q, k) * w_intra
    num = w_inter[..., None] * jnp.einsum('bhtd,bhde->bhte', q, C) + jnp.einsum('bhts,bhse->bhte', s, v)
    den = w_inter * jnp.einsum('bhtd,bhd->bht', q, n) + jnp.sum(s, axis=-1)
    h = num / jnp.maximum(jnp.abs(den), jnp.exp(-m_t))[..., None]
    bL = b[..., -1]
    log_ws = bL[..., None] - b + ig
    m_new = jnp.maximum(bL + m, jnp.max(log_ws, axis=-1))
    decay = jnp.exp(bL + m - m_new)
    ws = jnp.exp(log_ws - m_new[..., None])
    C_new = decay[..., None, None] * C + jnp.einsum('bhs,bhsd,bhse->bhde', ws, k, v)
    n_new = decay[..., None] * n + jnp.einsum('bhs,bhsd->bhd', ws, k)
    return (C_new, n_new, m_new), h


def mlstm_scan(q, k, v, ig, lf):
    B, H, S, _ = q.shape
    nC = S // CHUNK

    def chunks(t):
        return jnp.moveaxis(t.reshape(t.shape[:2] + (nC, CHUNK) + t.shape[3:]), 2, 0)

    init = (jnp.zeros((B, H, ML_DH, ML_DH), jnp.float32),
            jnp.zeros((B, H, ML_DH), jnp.float32),
            jnp.zeros((B, H), jnp.float32))
    state, hs = lax.scan(lambda carry, xs: mlstm_chunk(carry, *xs), init,
                         (chunks(q), chunks(k), chunks(v), chunks(ig), chunks(lf)))
    h = jnp.moveaxis(hs, 0, 2).reshape(B, H, S, ML_DH)
    return h, state


def spatial_gating(u, v, ln_g, ln_b, w_s, b_s, L):
    B, S, _ = v.shape
    vn = layer_norm(v, ln_g, ln_b)
    vc = vn.reshape(B, S // L, L, SG_GROUPS, SG_GDIM)
    w = jnp.where(jnp.tril(jnp.ones((L, L), bool)), w_s[:, :L, :L], 0)
    z = jnp.einsum('gpr,bnrgc->bnpgc', w, vc) + b_s[:, :L].T[None, None, :, :, None]
    return u * z.reshape(B, S, SG_W), vn


def sink_softmax(s, sink):
    mx = jnp.maximum(jnp.max(s, axis=-1, keepdims=True), sink)
    p = jnp.exp(s - mx)
    return p / (jnp.sum(p, axis=-1, keepdims=True) + jnp.exp(sink - mx))


def swa_prompt(q, k, v, sinks):
    B, S = q.shape[:2]
    nC = S // CHUNK
    pad = WIN_CHUNKS * CHUNK

    def bands(t):
        tp = jnp.pad(t, ((0, 0), (pad, 0), (0, 0), (0, 0))).reshape(B, nC + WIN_CHUNKS, CHUNK, SW_KV, SW_DH)
        return jnp.concatenate([tp[:, j:j + nC] for j in range(WIN_CHUNKS + 1)], axis=2)

    kb, vb = bands(k), bands(v)
    qb = q.reshape(B, nC, CHUNK, SW_KV, SW_REP, SW_DH)
    s = jnp.einsum('bcqgrd,bckgd->bcgrqk', qb, kb).astype(jnp.float32) * (SW_DH ** -0.5)
    key_chunk = jnp.arange(nC)[:, None] - WIN_CHUNKS + jnp.arange(BAND)[None, :] // CHUNK
    s = jnp.where((key_chunk >= 0)[None, :, None, None, None, :], s, -jnp.inf)
    sink = sinks.astype(jnp.float32).reshape(SW_KV, SW_REP)[None, None, :, :, None, None]
    p = sink_softmax(s, sink).astype(v.dtype)
    return jnp.einsum('bcgrqk,bckgd->bcqgrd', p, vb).reshape(B, S, SW_W)


def swa_sample(q, k, v, cache_k, cache_v, sinks):
    B, T = q.shape[:2]
    kf = jnp.concatenate([cache_k.astype(k.dtype), k], axis=1)
    vf = jnp.concatenate([cache_v.astype(v.dtype), v], axis=1)
    qb = q.reshape(B, T, SW_KV, SW_REP, SW_DH)
    s = jnp.einsum('bqgrd,bkgd->bgrqk', qb, kf).astype(jnp.float32) * (SW_DH ** -0.5)
    sink = sinks.astype(jnp.float32).reshape(SW_KV, SW_REP)[None, :, :, None, None]
    p = sink_softmax(s, sink).astype(v.dtype)
    return jnp.einsum('bgrqk,bkgd->bqgrd', p, vf).reshape(B, T, SW_W)


def block(x, c, p, pos, st):
    B, S, _ = x.shape
    mod = jax.nn.silu(c) @ p['w_ada'] + p['b_ada']
    sh1, sc1, g1, sh2, sc2, g2 = [t[:, None, :] for t in jnp.split(mod, 6, axis=-1)]
    h = rms_norm(x, p['g_pre_mix']) * (1 + sc1) + sh1
    mq, mk, mv, mo, mi, mf, su, sv, wq, wk, wv, gl = split_in(h @ p['w_in'])

    def heads(t):
        return t.reshape(B, S, ML_HEADS, ML_DH).transpose(0, 2, 1, 3).astype(jnp.float32)

    q = heads(mq)
    k = heads(mk) * (ML_DH ** -0.5)
    v = heads(mv)
    ig = (mi.astype(jnp.float32) + p['b_igate'].astype(jnp.float32)).transpose(0, 2, 1)
    lf = jax.nn.log_sigmoid(mf.astype(jnp.float32) + p['b_fgate'].astype(jnp.float32)).transpose(0, 2, 1)
    if st is None:
        h_ml, (C, n, m) = mlstm_scan(q, k, v, ig, lf)
    else:
        carry = (st[0].astype(jnp.float32), st[1].astype(jnp.float32), st[2].astype(jnp.float32))
        (C, n, m), h_ml = mlstm_chunk(carry, q, k, v, ig, lf)
    h_ml = h_ml.transpose(0, 2, 1, 3).reshape(B, S, ML_W).astype(x.dtype)
    y_ml = jax.nn.sigmoid(mo) * h_ml

    L = SG_CHUNK if st is None else S
    y_sg, v_norm = spatial_gating(su, sv, p['ln_v_g'], p['ln_v_b'], p['w_spatial'], p['b_spatial'], L)

    qs = rope(wq.reshape(B, S, SW_HEADS, SW_DH), pos)
    ks = rope(wk.reshape(B, S, SW_KV, SW_DH), pos)
    vs = wv.reshape(B, S, SW_KV, SW_DH)
    if st is None:
        y_sw = swa_prompt(qs, ks, vs, p['swa_sinks'])
        new_k, new_v = ks[:, -WINDOW:], vs[:, -WINDOW:]
    else:
        y_sw = swa_sample(qs, ks, vs, st[3], st[4], p['swa_sinks'])
        new_k, new_v = ks, vs

    gates = jax.nn.sigmoid(gl).reshape(B, S, N_BRANCH, D_MODEL)
    merged = (gates[:, :, 0] * (y_ml @ p['w_br_mlstm'])
              + gates[:, :, 1] * (y_sg @ p['w_br_gmlp'])
              + gates[:, :, 2] * (y_sw @ p['w_br_swa']))
    x = x + g1 * rms_norm(merged @ p['w_out'], p['g_post_mix'])

    h2 = rms_norm(x, p['g_pre_ffn']) * (1 + sc2) + sh2
    gate, up = jnp.split(h2 @ p['w_ffn_in'], 2, axis=-1)
    f = (jax.nn.silu(gate) * up) @ p['w_ffn_out']
    x = x + g2 * rms_norm(f, p['g_post_ffn'])
    return x, (C, n, m, new_k, new_v, v_norm)


def setup_inputs(seed: int = 0) -> dict:
    key = jax.random.key(seed)
    ks = jax.random.split(key, 32)
    nrm = jax.random.normal
    f32 = jnp.float32
    wc = min(WINDOW, PAST_LEN)
    return {
        'x_prompt': nrm(ks[0], (BATCH, SEQ, D_MODEL), f32),
        'x_sample': nrm(ks[1], (DEC_BATCH, DEC_SEQ, D_MODEL), f32),
        'c_prompt': nrm(ks[2], (BATCH, D_MODEL), f32),
        'c_sample': nrm(ks[3], (DEC_BATCH, D_MODEL), f32),
        'state_mlstm_C': nrm(ks[4], (DEPTH, DEC_BATCH, ML_HEADS, ML_DH, ML_DH), f32) * ML_DH ** -0.5,
        'state_mlstm_n': nrm(ks[5], (DEPTH, DEC_BATCH, ML_HEADS, ML_DH), f32) * 0.5,
        'state_mlstm_m': jax.random.uniform(ks[6], (DEPTH, DEC_BATCH, ML_HEADS), f32, -1.0, 1.0),
        'cache_swa_k': nrm(ks[7], (DEPTH, DEC_BATCH, wc, SW_KV, SW_DH), f32),
        'cache_swa_v': nrm(ks[8], (DEPTH, DEC_BATCH, wc, SW_KV, SW_DH), f32),
        'w_ada': nrm(ks[9], (DEPTH, D_MODEL, 6 * D_MODEL), f32) * 0.5 * D_MODEL ** -0.5,
        'b_ada': nrm(ks[10], (DEPTH, 6 * D_MODEL), f32) * 0.02,
        'g_pre_mix': 1.0 + 0.02 * nrm(ks[11], (DEPTH, D_MODEL), f32),
        'g_post_mix': 1.0 + 0.02 * nrm(ks[12], (DEPTH, D_MODEL), f32),
        'g_pre_ffn': 1.0 + 0.02 * nrm(ks[13], (DEPTH, D_MODEL), f32),
        'g_post_ffn': 1.0 + 0.02 * nrm(ks[14], (DEPTH, D_MODEL), f32),
        'w_in': nrm(ks[15], (DEPTH, D_MODEL, D_IN), f32) * D_MODEL ** -0.5,
        'b_igate': 0.1 * nrm(ks[16], (DEPTH, ML_HEADS), f32),
        'b_fgate': jnp.linspace(3.0, 6.0, ML_HEADS, dtype=f32)[None, :] + 0.1 * nrm(ks[17], (DEPTH, ML_HEADS), f32),
        'ln_v_g': 1.0 + 0.02 * nrm(ks[18], (DEPTH, SG_W), f32),
        'ln_v_b': 0.02 * nrm(ks[19], (DEPTH, SG_W), f32),
        'w_spatial': nrm(ks[20], (DEPTH, SG_GROUPS, SG_CHUNK, SG_CHUNK), f32) * 0.5 * SG_CHUNK ** -0.5,
        'b_spatial': 1.0 + 0.02 * nrm(ks[21], (DEPTH, SG_GROUPS, SG_CHUNK), f32),
        'swa_sinks': 0.5 * nrm(ks[22], (DEPTH, SW_HEADS), f32),
        'w_br_mlstm': nrm(ks[23], (DEPTH, ML_W, D_MODEL), f32) * ML_W ** -0.5,
        'w_br_gmlp': nrm(ks[24], (DEPTH, SG_W, D_MODEL), f32) * SG_W ** -0.5,
        'w_br_swa': nrm(ks[25], (DEPTH, SW_W, D_MODEL), f32) * SW_W ** -0.5,
        'w_out': nrm(ks[26], (DEPTH, D_MODEL, D_MODEL), f32) * D_MODEL ** -0.5,
        'w_ffn_in': nrm(ks[27], (DEPTH, D_MODEL, 2 * FFN_HIDDEN), f32) * D_MODEL ** -0.5,
        'w_ffn_out': nrm(ks[28], (DEPTH, FFN_HIDDEN, D_MODEL), f32) * FFN_HIDDEN ** -0.5,
    }


def reference(x_prompt, x_sample, c_prompt, c_sample, state_mlstm_C, state_mlstm_n, state_mlstm_m,
              cache_swa_k, cache_swa_v, w_ada, b_ada, g_pre_mix, g_post_mix, g_pre_ffn, g_post_ffn,
              w_in, b_igate, b_fgate, ln_v_g, ln_v_b, w_spatial, b_spatial, swa_sinks,
              w_br_mlstm, w_br_gmlp, w_br_swa, w_out, w_ffn_in, w_ffn_out):
    pos_p = jnp.arange(x_prompt.shape[1])
    pos_s = PAST_LEN + jnp.arange(x_sample.shape[1])
    xp, xs = x_prompt, x_sample
    new_p = [[] for _ in range(5)]
    new_s = [[] for _ in range(6)]
    for l in range(DEPTH):
        p = dict(w_ada=w_ada[l], b_ada=b_ada[l], g_pre_mix=g_pre_mix[l], g_post_mix=g_post_mix[l],
                 g_pre_ffn=g_pre_ffn[l], g_post_ffn=g_post_ffn[l], w_in=w_in[l], b_igate=b_igate[l],
                 b_fgate=b_fgate[l], ln_v_g=ln_v_g[l], ln_v_b=ln_v_b[l], w_spatial=w_spatial[l],
                 b_spatial=b_spatial[l], swa_sinks=swa_sinks[l], w_br_mlstm=w_br_mlstm[l],
                 w_br_gmlp=w_br_gmlp[l], w_br_swa=w_br_swa[l], w_out=w_out[l],
                 w_ffn_in=w_ffn_in[l], w_ffn_out=w_ffn_out[l])
        xp, st_p = block(xp, c_prompt, p, pos_p, None)
        xs, st_s = block(xs, c_sample, p, pos_s,
                         (state_mlstm_C[l], state_mlstm_n[l], state_mlstm_m[l], cache_swa_k[l], cache_swa_v[l]))
        for lst, a in zip(new_p, st_p[:5]):
            lst.append(a)
        for lst, a in zip(new_s, st_s):
            lst.append(a)
    C_p, n_p, m_p, k_p, v_p = [jnp.stack(a) for a in new_p]
    C_s, n_s, m_s, k_s, v_s, gv_s = [jnp.stack(a) for a in new_s]
    return (xp, xs, C_p, n_p, m_p, k_p, v_p, C_s, n_s, m_s, k_s, v_s, gv_s)
```

```python
import functools

import numpy as np
import jax
import jax.numpy as jnp
from jax import lax
from jax.experimental import pallas as pl
from jax.experimental.pallas import tpu as pltpu

F32 = jnp.float32
BF16 = jnp.bfloat16

D_MODEL = 1024
DEPTH = 4
PAST_LEN = 1024
ML_HEADS = 4
ML_DH = 128
ML_W = ML_HEADS * ML_DH
SG_GROUPS = 4
SG_CHUNK = 128
SG_W = 512
SG_GDIM = SG_W // SG_GROUPS
SW_HEADS = 8
SW_KV = 2
SW_REP = SW_HEADS // SW_KV
SW_DH = 64
SW_W = SW_HEADS * SW_DH
SW_KVW = SW_KV * SW_DH
SW_CHUNK = 64
WINDOW = 128
BAND = WINDOW + SW_CHUNK
ROPE_THETA = 10000.0
FFN_HIDDEN = 2816
EPS = 1e-6
SPLIT_SIZES = (ML_W, ML_W, ML_W, ML_W, ML_HEADS, ML_HEADS, SG_W, SG_W, SW_W, SW_KVW, SW_KVW, 3 * D_MODEL)

LANES = 128
N_GATE = 2 * ML_HEADS

OFF_MQ = 0
OFF_MK = OFF_MQ + ML_W
OFF_MV = OFF_MK + ML_W
OFF_MO = OFF_MV + ML_W
OFF_SU = OFF_MO + ML_W
OFF_SV = OFF_SU + SG_W
OFF_WQ = OFF_SV + SG_W
OFF_WK = OFF_WQ + SW_W
OFF_WV = OFF_WK + SW_KVW
OFF_GL = OFF_WV + SW_KVW
OFF_GATE = OFF_GL + 3 * D_MODEL
D_IN_PAD = OFF_GATE + LANES

FFN_SPLITS = ((0, 1536), (1536, FFN_HIDDEN))

VMEM_LIMIT = 56 * 1024 * 1024

NT_DIMS = (((1,), (1,)), ((), ()))
TN_DIMS = (((0,), (0,)), ((), ()))


def _dot(a, b):
    return jnp.dot(a, b, preferred_element_type=F32)


def _dot_exact(a, b):
    return jnp.dot(a, b, preferred_element_type=F32, precision=lax.Precision.HIGHEST)


def _rms(x):
    return x * lax.rsqrt(jnp.mean(x * x, axis=-1, keepdims=True) + EPS)


def _rows(m3, R, Lr):
    W = m3.shape[-1]
    if R == 1:
        return m3.reshape(1, W)
    return jnp.broadcast_to(m3, (R, Lr, W)).reshape(R * Lr, W)


def _mod_kernel(c_ref, w_ref, b_ref, o_ref):
    c = c_ref[...]
    a = (c * jax.nn.sigmoid(c)).astype(BF16)
    o_ref[...] = _dot(a, w_ref[...].astype(BF16)) + b_ref[...]


def _modulation(c_all, w_ada, b_ada):
    nb = c_all.shape[0]
    return pl.pallas_call(
        _mod_kernel,
        grid=(DEPTH, 6),
        in_specs=[
            pl.BlockSpec((nb, D_MODEL), lambda l, j: (0, 0)),
            pl.BlockSpec((None, D_MODEL, D_MODEL), lambda l, j: (l, 0, j)),
            pl.BlockSpec((None, None, 1, D_MODEL), lambda l, j: (l, j, 0, 0)),
        ],
        out_specs=pl.BlockSpec((None, None, nb, D_MODEL), lambda l, j: (l, j, 0, 0)),
        out_shape=jax.ShapeDtypeStruct((DEPTH, 6, nb, D_MODEL), F32),
        compiler_params=pltpu.CompilerParams(dimension_semantics=("parallel", "parallel")),
        name="adaln_modulation",
    )(c_all, w_ada, b_ada.reshape(DEPTH, 6, 1, D_MODEL))


def _rope(x, cos, sin_signed):
    T, W = x.shape
    lane = lax.broadcasted_iota(jnp.int32, (T, LANES), 1)
    first_half = (lane & (SW_DH // 2)) == 0
    outs = []
    for j in range(W // LANES):
        xb = x[:, j * LANES:(j + 1) * LANES]
        swapped = jnp.where(first_half,
                            pltpu.roll(xb, LANES - SW_DH // 2, 1),
                            pltpu.roll(xb, SW_DH // 2, 1))
        outs.append(xb * cos + swapped * sin_signed)
    return outs[0] if len(outs) == 1 else jnp.concatenate(outs, axis=1)


def _mlstm_unit(q, k, v, bcol, brow, ig_col, ig_row, caug, m_prev):
    L = q.shape[0]
    ti = lax.broadcasted_iota(jnp.int32, (L, L), 0)
    si = lax.broadcasted_iota(jnp.int32, (L, L), 1)
    logd = jnp.where(ti >= si, bcol - brow + ig_row, -jnp.inf)
    inter = bcol + m_prev
    m_t = jnp.maximum(inter, jnp.max(logd, axis=-1, keepdims=True))
    w_intra = jnp.exp(logd - m_t)
    w_inter = jnp.exp(inter - m_t)
    qb = q.astype(BF16)
    kb = k.astype(BF16)
    s = lax.dot_general(qb, kb, NT_DIMS, preferred_element_type=F32) * w_intra
    lane = lax.broadcasted_iota(jnp.int32, (L, LANES), 1)
    ones_col = jnp.where(lane == 0, 1.0, 0.0).astype(F32)
    vaug = jnp.concatenate([v, ones_col], axis=1)
    qc = _dot(qb, caug.astype(BF16))
    sv = _dot(s.astype(BF16), vaug.astype(BF16))
    num = w_inter * qc[:, :ML_DH] + sv[:, :ML_DH]
    den = w_inter * qc[:, ML_DH:ML_DH + 1] + sv[:, ML_DH:ML_DH + 1]
    h = num / jnp.maximum(jnp.abs(den), jnp.exp(-m_t))
    b_last = bcol[L - 1:L, :]
    log_ws = b_last - bcol + ig_col
    m_new = jnp.maximum(b_last + m_prev, jnp.max(log_ws, axis=0, keepdims=True))
    decay = jnp.exp(b_last + m_prev - m_new)
    ws = jnp.exp(log_ws - m_new)
    upd = lax.dot_general(kb, (ws * vaug).astype(BF16), TN_DIMS, preferred_element_type=F32)
    return h, decay * caug + upd, m_new


def _swa_unit(qs, kb, vb, sink_col, mask):
    s = lax.dot_general(qs, kb, NT_DIMS, preferred_element_type=F32)
    if mask is not None:
        s = jnp.where(mask, s, -jnp.inf)
    mx = jnp.maximum(jnp.max(s, axis=-1, keepdims=True), sink_col)
    p = jnp.exp(s - mx)
    den = jnp.sum(p, axis=-1, keepdims=True) + jnp.exp(sink_col - mx)
    p = p * (1.0 / den)
    return _dot(p.astype(BF16), vb)


def _mixer_kernel(*refs, R, Lr, Lg, has_state, n_seq):
    it = iter(refs)
    x_ref, sh1_ref, sc1_ref, g1_ref = next(it), next(it), next(it), next(it)
    win_ref, wgt_ref, gpre_ref, gpost_ref = next(it), next(it), next(it), next(it)
    bgc_ref, bgr_ref = next(it), next(it)
    lng_ref, lnb_ref, wsp_ref, bsp_ref = next(it), next(it), next(it), next(it)
    sinks_ref = next(it)
    wbm_ref, wbg_ref, wbs_ref, wout_ref = next(it), next(it), next(it), next(it)
    cos_ref, sin_ref = next(it), next(it)
    if has_state:
        stin_ref, min_ref, ck_ref, cv_ref = next(it), next(it), next(it), next(it)
    xo_ref, st_ref, m_ref, kn_ref, vn_ref = next(it), next(it), next(it), next(it), next(it)
    if has_state:
        gv_ref = next(it)
    yml_ref, ysg_ref, ysw_ref = next(it), next(it), next(it)
    if not has_state:
        kprev_ref, vprev_ref = next(it), next(it)

    T = R * Lr
    seq = pl.program_id(1)

    @pl.when(seq == 0)
    def _init():
        if has_state:
            st_ref[...] = stin_ref[...]
            m_ref[...] = min_ref[...]
        else:
            st_ref[...] = jnp.zeros_like(st_ref)
            m_ref[...] = jnp.zeros_like(m_ref)
            kprev_ref[...] = jnp.zeros_like(kprev_ref)
            vprev_ref[...] = jnp.zeros_like(vprev_ref)

    x = x_ref[...].reshape(T, D_MODEL)
    h = _rms(x) * gpre_ref[...] * (1.0 + _rows(sc1_ref[...], R, Lr)) + _rows(sh1_ref[...], R, Lr)
    hb = h.astype(BF16)

    def proj(lo, width):
        return _dot(hb, win_ref[:, lo:lo + width])

    gcol = proj(OFF_GATE, LANES) + bgc_ref[...]
    grow = lax.dot_general(wgt_ref[...], hb, NT_DIMS, preferred_element_type=F32) + bgr_ref[...]
    lf_col = jax.nn.log_sigmoid(gcol)
    lf_row = jax.nn.log_sigmoid(grow)
    ri = lax.broadcasted_iota(jnp.int32, (Lr, Lr), 0)
    ci = lax.broadcasted_iota(jnp.int32, (Lr, Lr), 1)
    tril = jnp.where(ri >= ci, 1.0, 0.0).astype(F32)
    triu = jnp.where(ri <= ci, 1.0, 0.0).astype(F32)
    mq = proj(OFF_MQ, ML_W)
    mk = proj(OFF_MK, ML_W) * (ML_DH ** -0.5)
    mv = proj(OFF_MV, ML_W)
    mo = jax.nn.sigmoid(proj(OFF_MO, ML_W))
    for r in range(R):
        rs = slice(r * Lr, (r + 1) * Lr)
        bcol_all = _dot_exact(tril, lf_col[rs, :])
        brow_all = _dot_exact(lf_row[:, rs], triu)
        for hd in range(ML_HEADS):
            hs = slice(hd * ML_DH, (hd + 1) * ML_DH)
            fg = ML_HEADS + hd
            h_ml, caug_new, m_new = _mlstm_unit(
                mq[rs, hs], mk[rs, hs], mv[rs, hs],
                bcol_all[:, fg:fg + 1], brow_all[fg:fg + 1, :],
                gcol[rs, hd:hd + 1], grow[hd:hd + 1, rs],
                st_ref[r, hd], m_ref[r, hd:hd + 1, 0:1])
            st_ref[r, hd] = caug_new
            m_ref[r, hd:hd + 1, :] = jnp.broadcast_to(m_new, (1, LANES))
            yml_ref[rs, hs] = (mo[rs, hs] * h_ml).astype(BF16)

    su = proj(OFF_SU, SG_W)
    sv = proj(OFF_SV, SG_W)
    mu = jnp.mean(sv, axis=-1, keepdims=True)
    svc = sv - mu
    var = jnp.mean(svc * svc, axis=-1, keepdims=True)
    vnorm = svc * lax.rsqrt(var + EPS) * lng_ref[...] + lnb_ref[...]
    if has_state:
        gv_ref[...] = vnorm.reshape(R, Lr, SG_W)
    n_chunk = T // Lg
    gi = lax.broadcasted_iota(jnp.int32, (Lg, Lg), 0)
    gj = lax.broadcasted_iota(jnp.int32, (Lg, Lg), 1)
    for g in range(SG_GROUPS):
        cs = slice(g * SG_GDIM, (g + 1) * SG_GDIM)
        w = jnp.where(gi >= gj, wsp_ref[g, :Lg, :Lg], 0.0).astype(BF16)
        vcat = jnp.concatenate([vnorm[c * Lg:(c + 1) * Lg, cs] for c in range(n_chunk)], axis=1)
        z = _dot(w, vcat.astype(BF16)) + bsp_ref[:Lg, g:g + 1]
        for c in range(n_chunk):
            ts = slice(c * Lg, (c + 1) * Lg)
            ysg_ref[ts, cs] = (su[ts, cs] * z[:, c * SG_GDIM:(c + 1) * SG_GDIM]).astype(BF16)

    cos = cos_ref[...]
    sin = sin_ref[...]
    if R > 1:
        cos = jnp.concatenate([cos] * R, axis=0)
        sin = jnp.concatenate([sin] * R, axis=0)
    wq = _rope(proj(OFF_WQ, SW_W), cos, sin) * (SW_DH ** -0.5)
    wk = _rope(proj(OFF_WK, SW_KVW), cos, sin)
    wv = proj(OFF_WV, SW_KVW)
    wqb = wq.astype(BF16)
    n_sw = Lr // SW_CHUNK
    rid = lax.broadcasted_iota(jnp.int32, (SW_REP * SW_CHUNK, 1), 0) // SW_CHUNK
    kpos_chunk = lax.broadcasted_iota(jnp.int32, (SW_REP * SW_CHUNK, BAND), 1) // SW_CHUNK
    for r in range(R):
        if has_state:
            kband = jnp.concatenate([ck_ref[r], wk[r * Lr:(r + 1) * Lr]], axis=0).astype(BF16)
            vband = jnp.concatenate([cv_ref[r], wv[r * Lr:(r + 1) * Lr]], axis=0).astype(BF16)
        else:
            kband = jnp.concatenate([kprev_ref[...], wk], axis=0).astype(BF16)
            vband = jnp.concatenate([vprev_ref[...], wv], axis=0).astype(BF16)
        for g in range(SW_KV):
            kg = kband[:, g * SW_DH:(g + 1) * SW_DH]
            vg = vband[:, g * SW_DH:(g + 1) * SW_DH]
            sink_col = jnp.full((SW_REP * SW_CHUNK, 1), sinks_ref[SW_REP * g + SW_REP - 1], F32)
            for i in range(SW_REP - 2, -1, -1):
                sink_col = jnp.where(rid == i, sinks_ref[SW_REP * g + i], sink_col)
            for j in range(n_sw):
                t0 = r * Lr + j * SW_CHUNK
                qs = jnp.concatenate(
                    [wqb[t0:t0 + SW_CHUNK, (SW_REP * g + i) * SW_DH:(SW_REP * g + i + 1) * SW_DH]
                     for i in range(SW_REP)], axis=0)
                if has_state:
                    mask = None
                else:
                    mask = kpos_chunk >= (WINDOW // SW_CHUNK) - (seq * n_sw + j)
                o = _swa_unit(qs, kg[j * SW_CHUNK:j * SW_CHUNK + BAND],
                              vg[j * SW_CHUNK:j * SW_CHUNK + BAND], sink_col, mask)
                o4 = jnp.concatenate([o[i * SW_CHUNK:(i + 1) * SW_CHUNK] for i in range(SW_REP)], axis=1)
                ysw_ref[t0:t0 + SW_CHUNK, g * SW_REP * SW_DH:(g + 1) * SW_REP * SW_DH] = o4.astype(BF16)
    if has_state:
        kn_ref[...] = wk.reshape(R, Lr, SW_KVW)
        vn_ref[...] = wv.reshape(R, Lr, SW_KVW)
    else:
        kprev_ref[...] = wk[T - WINDOW:]
        vprev_ref[...] = wv[T - WINDOW:]

        @pl.when(seq == n_seq - 1)
        def _emit_cache():
            kn_ref[0] = wk[T - WINDOW:]
            vn_ref[0] = wv[T - WINDOW:]

    merged = jax.nn.sigmoid(proj(OFF_GL, D_MODEL)) * _dot(yml_ref[...], wbm_ref[...])
    merged = merged + jax.nn.sigmoid(proj(OFF_GL + D_MODEL, D_MODEL)) * _dot(ysg_ref[...], wbg_ref[...])
    merged = merged + jax.nn.sigmoid(proj(OFF_GL + 2 * D_MODEL, D_MODEL)) * _dot(ysw_ref[...], wbs_ref[...])
    o = _dot(merged.astype(BF16), wout_ref[...])
    y = x + _rows(g1_ref[...], R, Lr) * (_rms(o) * gpost_ref[...])
    xo_ref[...] = y.reshape(R, Lr, D_MODEL)


def _const_spec(shape, layer):
    nd = len(shape)
    return pl.BlockSpec((None,) + tuple(shape), lambda b, s: (layer,) + (0,) * nd,
                        pipeline_mode=pl.Buffered(1))


def _mixer(x, mod, b_off, layer, wts, cos_t, sin_t, state, *, R, Lr, Lg):
    B, S, _ = x.shape
    n_seq = S // Lr
    has_state = state is not None
    T = R * Lr

    def mod_spec(j):
        return pl.BlockSpec((None, None, R, 1, D_MODEL), lambda b, s: (layer, j, b + b_off // R, 0, 0))

    in_specs = [
        pl.BlockSpec((R, Lr, D_MODEL), lambda b, s: (b, s, 0)),
        mod_spec(0), mod_spec(1), mod_spec(2),
        _const_spec((D_MODEL, D_IN_PAD), layer),
        _const_spec((N_GATE, D_MODEL), layer),
        _const_spec((1, D_MODEL), layer),
        _const_spec((1, D_MODEL), layer),
        _const_spec((1, LANES), layer),
        _const_spec((N_GATE, 1), layer),
        _const_spec((1, SG_W), layer),
        _const_spec((1, SG_W), layer),
        _const_spec((SG_GROUPS, SG_CHUNK, SG_CHUNK), layer),
        _const_spec((SG_CHUNK, SG_GROUPS), layer),
        pl.BlockSpec(memory_space=pltpu.SMEM),
        _const_spec((ML_W, D_MODEL), layer),
        _const_spec((SG_W, D_MODEL), layer),
        _const_spec((SW_W, D_MODEL), layer),
        _const_spec((D_MODEL, D_MODEL), layer),
        pl.BlockSpec((Lr, LANES), (lambda b, s: (0, 0)) if has_state else (lambda b, s: (s, 0))),
        pl.BlockSpec((Lr, LANES), (lambda b, s: (0, 0)) if has_state else (lambda b, s: (s, 0))),
    ]
    args = [x, mod, mod, mod, wts["win"], wts["wgt"], wts["g_pre_mix"], wts["g_post_mix"],
            wts["bg_col"], wts["bg_row"], wts["ln_v_g"], wts["ln_v_b"], wts["w_spatial"], wts["b_spatial_t"],
            wts["swa_sinks"][layer], wts["w_br_mlstm"], wts["w_br_gmlp"], wts["w_br_swa"], wts["w_out"],
            cos_t, sin_t]
    n_new = Lr if has_state else WINDOW
    out_shape = [
        jax.ShapeDtypeStruct((B, S, D_MODEL), F32),
        jax.ShapeDtypeStruct((B, ML_HEADS, ML_DH, 2 * ML_DH), F32),
        jax.ShapeDtypeStruct((B, ML_HEADS, LANES), F32),
        jax.ShapeDtypeStruct((B, n_new, SW_KVW), F32),
        jax.ShapeDtypeStruct((B, n_new, SW_KVW), F32),
    ]
    out_specs = [
        pl.BlockSpec((R, Lr, D_MODEL), lambda b, s: (b, s, 0)),
        pl.BlockSpec((R, ML_HEADS, ML_DH, 2 * ML_DH), lambda b, s: (b, 0, 0, 0)),
        pl.BlockSpec((R, ML_HEADS, LANES), lambda b, s: (b, 0, 0)),
        pl.BlockSpec((R, n_new, SW_KVW), lambda b, s: (b, 0, 0)),
        pl.BlockSpec((R, n_new, SW_KVW), lambda b, s: (b, 0, 0)),
    ]
    scratch = [pltpu.VMEM((T, ML_W), BF16), pltpu.VMEM((T, SG_W), BF16), pltpu.VMEM((T, SW_W), BF16)]
    if has_state:
        caug_in, m_in, cache_k, cache_v = state
        in_specs += [
            pl.BlockSpec((R, ML_HEADS, ML_DH, 2 * ML_DH), lambda b, s: (b, 0, 0, 0)),
            pl.BlockSpec((R, ML_HEADS, LANES), lambda b, s: (b, 0, 0)),
            pl.BlockSpec((R, WINDOW, SW_KVW), lambda b, s: (b, 0, 0)),
            pl.BlockSpec((R, WINDOW, SW_KVW), lambda b, s: (b, 0, 0)),
        ]
        args += [caug_in, m_in, cache_k, cache_v]
        out_shape.append(jax.ShapeDtypeStruct((B, S, SG_W), F32))
        out_specs.append(pl.BlockSpec((R, Lr, SG_W), lambda b, s: (b, s, 0)))
    else:
        scratch += [pltpu.VMEM((WINDOW, SW_KVW), F32), pltpu.VMEM((WINDOW, SW_KVW), F32)]

    return pl.pallas_call(
        functools.partial(_mixer_kernel, R=R, Lr=Lr, Lg=Lg, has_state=has_state, n_seq=n_seq),
        grid=(B // R, n_seq),
        in_specs=in_specs,
        out_specs=out_specs,
        out_shape=out_shape,
        scratch_shapes=scratch,
        compiler_params=pltpu.CompilerParams(
            dimension_semantics=("parallel", "arbitrary"), vmem_limit_bytes=VMEM_LIMIT),
        name="mixer_sample" if has_state else "mixer_prompt",
    )(*args)


def _ffn_kernel(x_ref, sh_ref, sc_ref, g_ref, gpre_ref, gpost_ref, wfi_ref, wfo_ref, o_ref, *, R, Lr):
    T = R * Lr
    x = x_ref[...].reshape(T, D_MODEL)
    h = _rms(x) * gpre_ref[...] * (1.0 + _rows(sc_ref[...], R, Lr)) + _rows(sh_ref[...], R, Lr)
    hb = h.astype(BF16)
    f = None
    for lo, hi in FFN_SPLITS:
        gate = _dot(hb, wfi_ref[:, lo:hi])
        up = _dot(hb, wfi_ref[:, FFN_HIDDEN + lo:FFN_HIDDEN + hi])
        act = (gate * jax.nn.sigmoid(gate) * up).astype(BF16)
        part = _dot(act, wfo_ref[lo:hi, :])
        f = part if f is None else f + part
    y = x + _rows(g_ref[...], R, Lr) * (_rms(f) * gpost_ref[...])
    o_ref[...] = y.reshape(R, Lr, D_MODEL)


def _ffn(x, mod, b_off, layer, wts, *, R, Lr):
    B, S, _ = x.shape

    def mod_spec(j):
        return pl.BlockSpec((None, None, R, 1, D_MODEL), lambda b, s: (layer, j, b + b_off // R, 0, 0))

    return pl.pallas_call(
        functools.partial(_ffn_kernel, R=R, Lr=Lr),
        grid=(B // R, S // Lr),
        in_specs=[
            pl.BlockSpec((R, Lr, D_MODEL), lambda b, s: (b, s, 0)),
            mod_spec(3), mod_spec(4), mod_spec(5),
            _const_spec((1, D_MODEL), layer),
            _const_spec((1, D_MODEL), layer),
            _const_spec((D_MODEL, 2 * FFN_HIDDEN), layer),
            _const_spec((FFN_HIDDEN, D_MODEL), layer),
        ],
        out_specs=pl.BlockSpec((R, Lr, D_MODEL), lambda b, s: (b, s, 0)),
        out_shape=jax.ShapeDtypeStruct((B, S, D_MODEL), F32),
        compiler_params=pltpu.CompilerParams(
            dimension_semantics=("parallel", "parallel"), vmem_limit_bytes=VMEM_LIMIT),
        name="ffn",
    )(x, mod, mod, mod, wts["g_pre_ffn"], wts["g_post_ffn"], wts["w_ffn_in"], wts["w_ffn_out"])


def _rope_tables(pos):
    half = SW_DH // 2
    inv = ROPE_THETA ** (-jnp.arange(half, dtype=F32) / half)
    ang = pos.astype(F32)[:, None] * inv[None, :]
    cos, sin = jnp.cos(ang), jnp.sin(ang)
    reps = LANES // SW_DH
    cos_t = jnp.tile(jnp.concatenate([cos, cos], axis=1), (1, reps))
    sin_t = jnp.tile(jnp.concatenate([-sin, sin], axis=1), (1, reps))
    return cos_t, sin_t


def _prep_weights(w_in, b_igate, b_fgate, g_pre_mix, g_post_mix, g_pre_ffn, g_post_ffn, ln_v_g, ln_v_b,
                  w_spatial, b_spatial, swa_sinks, w_br_mlstm, w_br_gmlp, w_br_swa, w_out, w_ffn_in, w_ffn_out):
    offs = np.cumsum((0,) + SPLIT_SIZES)
    seg = [w_in[..., offs[i]:offs[i + 1]] for i in range(len(SPLIT_SIZES))]
    mq, mk, mv, mo, mi, mf, su, sv, wq, wk, wv, gl = seg
    gates = jnp.concatenate([mi, mf], axis=-1)
    gates_pad = jnp.pad(gates, ((0, 0), (0, 0), (0, LANES - N_GATE)))
    win = jnp.concatenate([mq, mk, mv, mo, su, sv, wq, wk, wv, gl, gates_pad], axis=-1).astype(BF16)
    bg = jnp.concatenate([b_igate, b_fgate], axis=-1)
    row = lambda a: a.reshape(DEPTH, 1, a.shape[-1])
    return dict(
        win=win,
        wgt=jnp.swapaxes(gates, 1, 2).astype(BF16),
        bg_col=jnp.pad(bg, ((0, 0), (0, LANES - N_GATE))).reshape(DEPTH, 1, LANES),
        bg_row=bg.reshape(DEPTH, N_GATE, 1),
        g_pre_mix=row(g_pre_mix), g_post_mix=row(g_post_mix),
        g_pre_ffn=row(g_pre_ffn), g_post_ffn=row(g_post_ffn),
        ln_v_g=row(ln_v_g), ln_v_b=row(ln_v_b),
        w_spatial=w_spatial, b_spatial_t=jnp.swapaxes(b_spatial, 1, 2),
        swa_sinks=swa_sinks,
        w_br_mlstm=w_br_mlstm.astype(BF16), w_br_gmlp=w_br_gmlp.astype(BF16),
        w_br_swa=w_br_swa.astype(BF16), w_out=w_out.astype(BF16),
        w_ffn_in=w_ffn_in.astype(BF16), w_ffn_out=w_ffn_out.astype(BF16),
    )


MIX_TOKENS_PROMPT = 256
MIX_ROWS_SAMPLE = 4
FFN_TOKENS = 512


def kernel(x_prompt, x_sample, c_prompt, c_sample, state_mlstm_C, state_mlstm_n, state_mlstm_m, cache_swa_k, cache_swa_v, w_ada, b_ada, g_pre_mix, g_post_mix, g_pre_ffn, g_post_ffn, w_in, b_igate, b_fgate, ln_v_g, ln_v_b, w_spatial, b_spatial, swa_sinks, w_br_mlstm, w_br_gmlp, w_br_swa, w_out, w_ffn_in, w_ffn_out):
    Bp, Sp, _ = x_prompt.shape
    Bs, Ss, _ = x_sample.shape
    wts = _prep_weights(w_in, b_igate, b_fgate, g_pre_mix, g_post_mix, g_pre_ffn, g_post_ffn, ln_v_g, ln_v_b,
                        w_spatial, b_spatial, swa_sinks, w_br_mlstm, w_br_gmlp, w_br_swa, w_out,
                        w_ffn_in, w_ffn_out)
    mod = _modulation(jnp.concatenate([c_prompt, c_sample], axis=0), w_ada, b_ada)
    mod = mod.reshape(DEPTH, 6, Bp + Bs, 1, D_MODEL)
    cos_p, sin_p = _rope_tables(jnp.arange(Sp))
    cos_s, sin_s = _rope_tables(PAST_LEN + jnp.arange(Ss))

    n_col = jnp.pad(state_mlstm_n[..., None], ((0, 0),) * 4 + ((0, ML_DH - 1),))
    caug_in = jnp.concatenate([state_mlstm_C, n_col], axis=-1)
    m_in = jnp.broadcast_to(state_mlstm_m[..., None], state_mlstm_m.shape + (LANES,))
    ck = cache_swa_k.reshape(DEPTH, Bs, WINDOW, SW_KVW)
    cv = cache_swa_v.reshape(DEPTH, Bs, WINDOW, SW_KVW)

    xp, xs = x_prompt, x_sample
    outs_p, outs_s = [], []
    for l in range(DEPTH):
        xp, caug_p, m_p, k_p, v_p = _mixer(xp, mod, 0, l, wts, cos_p, sin_p, None,
                                           R=1, Lr=MIX_TOKENS_PROMPT, Lg=SG_CHUNK)
        xp = _ffn(xp, mod, 0, l, wts, R=1, Lr=FFN_TOKENS)
        xs, caug_s, m_s, k_s, v_s, gv_s = _mixer(xs, mod, Bp, l, wts, cos_s, sin_s,
                                                 (caug_in[l], m_in[l], ck[l], cv[l]),
                                                 R=MIX_ROWS_SAMPLE, Lr=Ss, Lg=Ss)
        xs = _ffn(xs, mod, Bp, l, wts, R=FFN_TOKENS // Ss, Lr=Ss)
        outs_p.append((caug_p, m_p, k_p, v_p))
        outs_s.append((caug_s, m_s, k_s, v_s, gv_s))

    def unpack(outs, B, n_new):
        caug = jnp.stack([o[0] for o in outs])
        C = caug[..., :ML_DH]
        n = caug[..., ML_DH]
        m = jnp.stack([o[1] for o in outs])[..., 0]
        k = jnp.stack([o[2] for o in outs]).reshape(DEPTH, B, n_new, SW_KV, SW_DH)
        v = jnp.stack([o[3] for o in outs]).reshape(DEPTH, B, n_new, SW_KV, SW_DH)
        return C, n, m, k, v

    C_p, n_p, m_p, k_p, v_p = unpack(outs_p, Bp, WINDOW)
    C_s, n_s, m_s, k_s, v_s = unpack(outs_s, Bs, Ss)
    gv_s = jnp.stack([o[4] for o in outs_s])
    return (xp, xs, C_p, n_p, m_p, k_p, v_p, C_s, n_s, m_s, k_s, v_s, gv_s)
```

```python
import functools

import numpy as np
import jax
import jax.numpy as jnp
from jax import lax
from jax.experimental import pallas as pl
from jax.experimental.pallas import tpu as pltpu

F32 = jnp.float32
BF16 = jnp.bfloat16

D_MODEL = 1024
DEPTH = 4
PAST_LEN = 1024
ML_HEADS = 4
ML_DH = 128
ML_W = ML_HEADS * ML_DH
SG_GROUPS = 4
SG_CHUNK = 128
SG_W = 512
SG_GDIM = SG_W // SG_GROUPS
SW_HEADS = 8
SW_KV = 2
SW_REP = SW_HEADS // SW_KV
SW_DH = 64
SW_HALF = SW_DH // 2
SW_W = SW_HEADS * SW_DH
SW_KVW = SW_KV * SW_DH
SW_CHUNK = 64
WINDOW = 128
BAND = WINDOW + SW_CHUNK
ROPE_THETA = 10000.0
FFN_HIDDEN = 2816
EPS = 1e-6
SPLIT_SIZES = (ML_W, ML_W, ML_W, ML_W, ML_HEADS, ML_HEADS, SG_W, SG_W, SW_W, SW_KVW, SW_KVW, 3 * D_MODEL)

LANES = 128
N_GATE = 2 * ML_HEADS
GATE_ROWS = 16

OFF_MQ = 0
OFF_MV = OFF_MQ + ML_W
OFF_MO = OFF_MV + ML_W
OFF_SU = OFF_MO + ML_W
OFF_SV = OFF_SU + SG_W
OFF_WQ = OFF_SV + SG_W
OFF_WK = OFF_WQ + SW_W
OFF_WV = OFF_WK + SW_KVW
OFF_GL = OFF_WV + SW_KVW
D_IN_PAD = OFF_GL + 3 * D_MODEL

FFN_SPLITS = ((0, 1536), (1536, FFN_HIDDEN))

VMEM_LIMIT = 56 * 1024 * 1024

NT_DIMS = (((1,), (1,)), ((), ()))


def _dot(a, b):
    return jnp.dot(a, b, preferred_element_type=F32)


def _dot_nt(a, b):
    return lax.dot_general(a, b, NT_DIMS, preferred_element_type=F32)


def _sigmoid(x):
    return 0.5 * jnp.tanh(0.5 * x) + 0.5


def _rms(x):
    return x * lax.rsqrt(jnp.mean(x * x, axis=-1, keepdims=True) + EPS)


def _rows(m3, R, Lr):
    W = m3.shape[-1]
    if R == 1:
        return m3.reshape(1, W)
    return jnp.broadcast_to(m3, (R, Lr, W)).reshape(R * Lr, W)


def _split3(x):
    h1 = x.astype(BF16)
    r1 = x - h1.astype(F32)
    h2 = r1.astype(BF16)
    h3 = (r1 - h2.astype(F32)).astype(BF16)
    return h1, h2, h3


class _Fillers:
    def __init__(self, total_weight):
        self._thunks, self._vals, self._left = {}, {}, total_weight

    def add(self, name, thunk):
        self._thunks[name] = thunk

    def get(self, name):
        if name not in self._vals:
            self._vals[name] = self._thunks.pop(name)()
        return self._vals[name]

    def pump(self, weight):
        n = -(-len(self._thunks) * weight // max(self._left, 1))
        self._left -= weight
        for name in list(self._thunks)[:n]:
            self.get(name)


def _mod_kernel(c_ref, w_ref, b_ref, o_ref):
    c = c_ref[...]
    a = (c * _sigmoid(c)).astype(BF16)
    o_ref[...] = _dot(a, w_ref[...].astype(BF16)) + b_ref[...]


def _modulation(c_all, w_ada, b_ada):
    nb = c_all.shape[0]
    return pl.pallas_call(
        _mod_kernel,
        grid=(DEPTH, 6),
        in_specs=[
            pl.BlockSpec((nb, D_MODEL), lambda l, j: (0, 0)),
            pl.BlockSpec((None, D_MODEL, D_MODEL), lambda l, j: (l, 0, j)),
            pl.BlockSpec((None, None, 1, D_MODEL), lambda l, j: (l, j, 0, 0)),
        ],
        out_specs=pl.BlockSpec((None, None, nb, D_MODEL), lambda l, j: (l, j, 0, 0)),
        out_shape=jax.ShapeDtypeStruct((DEPTH, 6, nb, D_MODEL), F32),
        compiler_params=pltpu.CompilerParams(dimension_semantics=("parallel", "parallel")),
        name="adaln_modulation",
    )(c_all, w_ada, b_ada.reshape(DEPTH, 6, 1, D_MODEL))


def _rope(x, cos, sin_signed):
    T, W = x.shape
    outs = []
    for j in range(W // LANES):
        xb = x[:, j * LANES:(j + 1) * LANES]
        outs.append(xb * cos + pltpu.roll(xb, LANES // 2, 1) * sin_signed)
    return outs[0] if len(outs) == 1 else jnp.concatenate(outs, axis=1)


def _mlstm_unit(qb, kt, v, a_row, b_col, caug, m_prev):
    L = qb.shape[0]
    ti = lax.broadcasted_iota(jnp.int32, (L, L), 0)
    si = lax.broadcasted_iota(jnp.int32, (L, L), 1)
    a_tri = jnp.where(ti >= si, a_row, -jnp.inf)
    g = jnp.maximum(jnp.max(a_tri, axis=-1, keepdims=True), m_prev)
    w_intra = jnp.exp(a_tri - g)
    w_inter = jnp.exp(m_prev - g)
    s = _dot(qb, kt.astype(BF16)) * w_intra
    vaug = jnp.concatenate([v, jnp.ones_like(v)], axis=1).astype(BF16)
    qc = _dot(qb, caug.astype(BF16))
    sv = _dot(s.astype(BF16), vaug)
    num = w_inter * qc[:, :ML_DH] + sv[:, :ML_DH]
    den = w_inter * qc[:, ML_DH:] + sv[:, ML_DH:]
    floor = jnp.exp(-(jnp.broadcast_to(b_col, (L, ML_DH)) + g))
    h = num / jnp.maximum(jnp.abs(den), floor)
    g_last = g[L - 1:L, :]
    ws_row = jnp.exp(a_row - g_last)
    upd = _dot((kt * ws_row).astype(BF16), vaug)
    caug_new = jnp.exp(m_prev - g_last) * caug + upd
    return h, caug_new, b_col[L - 1:L, :] + g_last


def _swa_chunk(qstack, kg, vaug, sink_cols, mask):
    o = None
    e_cols = []
    for g in range(SW_KV):
        s = _dot_nt(qstack, kg[g])
        if mask is not None:
            s = jnp.where(mask, s, -jnp.inf)
        mx = jnp.maximum(jnp.max(s, axis=-1, keepdims=True), sink_cols[g])
        p = jnp.exp(s - mx).astype(BF16)
        e_cols.append(jnp.exp(sink_cols[g] - mx))
        og = _dot(p, vaug[g])
        o = og if o is None else o + og
    lane = lax.broadcasted_iota(jnp.int32, (qstack.shape[0], LANES), 1)
    e = jnp.where(lane < SW_DH, e_cols[0], e_cols[1])
    return o[:, :LANES] / (o[:, LANES:] + e)


def _mixer_kernel(*refs, R, Lr, Lg, has_state, n_seq):
    it = iter(refs)
    x_ref, sh1_ref, sc1_ref, g1_ref = next(it), next(it), next(it), next(it)
    win_ref, wkt_ref, wgt_ref, gpre_ref, gpost_ref = next(it), next(it), next(it), next(it), next(it)
    bgr_ref = next(it)
    lng_ref, lnb_ref, wsp_ref, bsp_ref = next(it), next(it), next(it), next(it)
    sinks_ref = next(it)
    wbm_ref, wbg_ref, wbs_ref, wout_ref = next(it), next(it), next(it), next(it)
    cos_ref, sin_ref = next(it), next(it)
    if has_state:
        stin_ref, min_ref, ck_ref, cv_ref = next(it), next(it), next(it), next(it)
    xo_ref, st_ref, m_ref, kn_ref, vn_ref = next(it), next(it), next(it), next(it), next(it)
    if has_state:
        gv_ref = next(it)
    yml_ref, ysg_ref, ysw_ref = next(it), next(it), next(it)
    if not has_state:
        kprev_ref, vprev_ref = next(it), next(it)

    T = R * Lr
    seq = pl.program_id(1)

    @pl.when(seq == 0)
    def _init():
        if has_state:
            st_ref[...] = stin_ref[...]
            m_ref[...] = min_ref[...]
        else:
            st_ref[...] = jnp.zeros_like(st_ref)
            m_ref[...] = jnp.zeros_like(m_ref)
            kprev_ref[...] = jnp.zeros_like(kprev_ref)
            vprev_ref[...] = jnp.zeros_like(vprev_ref)

    x = x_ref[...].reshape(T, D_MODEL)
    h = _rms(x) * gpre_ref[...] * (1.0 + _rows(sc1_ref[...], R, Lr)) + _rows(sh1_ref[...], R, Lr)
    hb = h.astype(BF16)

    def proj(lo, width):
        return _dot(hb, win_ref[:, lo:lo + width])

    n_sw = Lr // SW_CHUNK
    w_ml, w_sg, w_sw = (3, 1, 3) if Lr > SW_CHUNK else (1, 1, 1)
    fill = _Fillers(R * ML_HEADS * w_ml + SG_GROUPS * w_sg + R * n_sw * w_sw)

    def gmlp_inputs():
        sv = proj(OFF_SV, SG_W)
        mu = jnp.mean(sv, axis=-1, keepdims=True)
        svc = sv - mu
        var = jnp.mean(svc * svc, axis=-1, keepdims=True)
        return svc * lax.rsqrt(var + EPS) * lng_ref[...] + lnb_ref[...]

    def rope_tables():
        cos, sin = cos_ref[...], sin_ref[...]
        if R > 1:
            cos = jnp.concatenate([cos] * R, axis=0)
            sin = jnp.concatenate([sin] * R, axis=0)
        return cos, sin

    fill.add("mo", lambda: _sigmoid(proj(OFF_MO, ML_W)))
    fill.add("su", lambda: proj(OFF_SU, SG_W))
    fill.add("vnorm", gmlp_inputs)
    fill.add("wq", lambda: (_rope(proj(OFF_WQ, SW_W), *rope_tables()) * (SW_DH ** -0.5)).astype(BF16))
    fill.add("wk", lambda: _rope(proj(OFF_WK, SW_KVW), *rope_tables()))
    fill.add("wv", lambda: proj(OFF_WV, SW_KVW))
    half = D_MODEL // 2
    for i in range(6):
        fill.add(("gl", i), functools.partial(lambda i: _sigmoid(proj(OFF_GL + i * half, half)), i))

    grow = _dot_nt(wgt_ref[...], hb) + bgr_ref[...]
    lf_row = jax.nn.log_sigmoid(grow)
    ri = lax.broadcasted_iota(jnp.int32, (Lr, Lr), 0)
    ci = lax.broadcasted_iota(jnp.int32, (Lr, Lr), 1)
    triu = jnp.where(ri <= ci, 1.0, 0.0).astype(BF16)
    triu3 = jnp.concatenate([triu] * 3, axis=0)
    mqb = proj(OFF_MQ, ML_W).astype(BF16)
    mkt = _dot_nt(wkt_ref[...], hb) * (ML_DH ** -0.5)
    mv = proj(OFF_MV, ML_W)
    for r in range(R):
        rs = slice(r * Lr, (r + 1) * Lr)
        brow_all = _dot(jnp.concatenate(_split3(lf_row[:, rs]), axis=1), triu3)
        bcol_all = brow_all.T
        for hd in range(ML_HEADS):
            hs = slice(hd * ML_DH, (hd + 1) * ML_DH)
            fg = ML_HEADS + hd
            h_ml, caug_new, m_new = _mlstm_unit(
                mqb[rs, hs], mkt[hs, rs], mv[rs, hs],
                grow[hd:hd + 1, rs] - brow_all[fg:fg + 1, :], bcol_all[:, fg:fg + 1],
                st_ref[r, hd], m_ref[r, hd:hd + 1, 0:1])
            st_ref[r, hd] = caug_new
            m_ref[r, hd:hd + 1, :] = jnp.broadcast_to(m_new, (1, LANES))
            fill.pump(w_ml)
            yml_ref[rs, hs] = (fill.get("mo")[rs, hs] * h_ml).astype(BF16)
    fill.add("ml", lambda: _dot(yml_ref[...], wbm_ref[...]))

    vnorm = fill.get("vnorm")
    su = fill.get("su")
    if has_state:
        gv_ref[...] = vnorm.reshape(R, Lr, SG_W)
    n_chunk = T // Lg
    gi = lax.broadcasted_iota(jnp.int32, (Lg, Lg), 0)
    gj = lax.broadcasted_iota(jnp.int32, (Lg, Lg), 1)
    for g in range(SG_GROUPS):
        cs = slice(g * SG_GDIM, (g + 1) * SG_GDIM)
        w = jnp.where(gi >= gj, wsp_ref[g, :Lg, :Lg], 0.0).astype(BF16)
        vcat = jnp.concatenate([vnorm[c * Lg:(c + 1) * Lg, cs] for c in range(n_chunk)], axis=1)
        z = _dot(w, vcat.astype(BF16))
        for c in range(n_chunk):
            ts = slice(c * Lg, (c + 1) * Lg)
            zc = z[:, c * SG_GDIM:(c + 1) * SG_GDIM] + bsp_ref[g, :Lg, :]
            ysg_ref[ts, cs] = (su[ts, cs] * zc).astype(BF16)
        fill.pump(w_sg)
    fill.add("sg", lambda: _dot(ysg_ref[...], wbg_ref[...]))

    wqb, wk, wv = fill.get("wq"), fill.get("wk"), fill.get("wv")
    n_q = SW_REP * SW_CHUNK
    rid = lax.broadcasted_iota(jnp.int32, (n_q, 1), 0) // SW_CHUNK
    sink_cols = []
    for g in range(SW_KV):
        col = jnp.full((n_q, 1), sinks_ref[SW_REP * g + SW_REP - 1], F32)
        for i in range(SW_REP - 2, -1, -1):
            col = jnp.where(rid == i, sinks_ref[SW_REP * g + i], col)
        sink_cols.append(col)
    kpos_chunk = lax.broadcasted_iota(jnp.int32, (n_q, BAND), 1) // SW_CHUNK
    for r in range(R):
        if has_state:
            kband = jnp.concatenate([ck_ref[r], wk[r * Lr:(r + 1) * Lr]], axis=0)
            vband = jnp.concatenate([cv_ref[r], wv[r * Lr:(r + 1) * Lr]], axis=0)
        else:
            kband = jnp.concatenate([kprev_ref[...], wk], axis=0)
            vband = jnp.concatenate([vprev_ref[...], wv], axis=0)
        lane = lax.broadcasted_iota(jnp.int32, kband.shape, 1)
        k_group = (lane // SW_HALF) % SW_KV
        v_group = lane // SW_DH
        kg = [jnp.where(k_group == g, kband, 0.0).astype(BF16) for g in range(SW_KV)]
        vaug = [jnp.concatenate([jnp.where(v_group == g, vband, 0.0),
                                 jnp.where(v_group == g, 1.0, 0.0)], axis=1).astype(BF16)
                for g in range(SW_KV)]
        for j in range(n_sw):
            t0 = r * Lr + j * SW_CHUNK
            qstack = jnp.concatenate([wqb[t0:t0 + SW_CHUNK, i * LANES:(i + 1) * LANES]
                                      for i in range(SW_REP)], axis=0)
            needs_mask = not has_state and j < WINDOW // SW_CHUNK
            mask = kpos_chunk >= (WINDOW // SW_CHUNK) - (seq * n_sw + j) if needs_mask else None
            ks = slice(j * SW_CHUNK, j * SW_CHUNK + BAND)
            y = _swa_chunk(qstack, [k[ks] for k in kg], [v[ks] for v in vaug], sink_cols, mask)
            for i in range(SW_REP):
                ysw_ref[t0:t0 + SW_CHUNK, i * LANES:(i + 1) * LANES] = (
                    y[i * SW_CHUNK:(i + 1) * SW_CHUNK].astype(BF16))
            fill.pump(w_sw)
    if has_state:
        kn_ref[...] = wk.reshape(R, Lr, SW_KVW)
        vn_ref[...] = wv.reshape(R, Lr, SW_KVW)
    else:
        kprev_ref[...] = wk[T - WINDOW:]
        vprev_ref[...] = wv[T - WINDOW:]

        @pl.when(seq == n_seq - 1)
        def _emit_cache():
            kn_ref[0] = wk[T - WINDOW:]
            vn_ref[0] = wv[T - WINDOW:]

    gate = lambda b: jnp.concatenate([fill.get(("gl", 2 * b)), fill.get(("gl", 2 * b + 1))], axis=1)
    merged = gate(0) * fill.get("ml")
    merged = merged + gate(1) * fill.get("sg")
    merged = merged + gate(2) * _dot(ysw_ref[...], wbs_ref[...])
    o = _dot(merged.astype(BF16), wout_ref[...])
    y = x + _rows(g1_ref[...], R, Lr) * (_rms(o) * gpost_ref[...])
    xo_ref[...] = y.reshape(R, Lr, D_MODEL)


def _const_spec(shape, layer):
    nd = len(shape)
    return pl.BlockSpec((None,) + tuple(shape), lambda b, s: (layer,) + (0,) * nd,
                        pipeline_mode=pl.Buffered(1))


def _mixer(x, mod, b_off, layer, wts, cos_t, sin_t, state, *, R, Lr, Lg):
    B, S, _ = x.shape
    n_seq = S // Lr
    has_state = state is not None
    T = R * Lr

    def mod_spec(j):
        return pl.BlockSpec((None, None, R, 1, D_MODEL), lambda b, s: (layer, j, b + b_off // R, 0, 0))

    in_specs = [
        pl.BlockSpec((R, Lr, D_MODEL), lambda b, s: (b, s, 0)),
        mod_spec(0), mod_spec(1), mod_spec(2),
        _const_spec((D_MODEL, D_IN_PAD), layer),
        _const_spec((ML_W, D_MODEL), layer),
        _const_spec((GATE_ROWS, D_MODEL), layer),
        _const_spec((1, D_MODEL), layer),
        _const_spec((1, D_MODEL), layer),
        _const_spec((GATE_ROWS, 1), layer),
        _const_spec((1, SG_W), layer),
        _const_spec((1, SG_W), layer),
        _const_spec((SG_GROUPS, SG_CHUNK, SG_CHUNK), layer),
        _const_spec((SG_GROUPS, SG_CHUNK, SG_GDIM), layer),
        pl.BlockSpec(memory_space=pltpu.SMEM),
        _const_spec((ML_W, D_MODEL), layer),
        _const_spec((SG_W, D_MODEL), layer),
        _const_spec((SW_W, D_MODEL), layer),
        _const_spec((D_MODEL, D_MODEL), layer),
        pl.BlockSpec((Lr, LANES), (lambda b, s: (0, 0)) if has_state else (lambda b, s: (s, 0))),
        pl.BlockSpec((Lr, LANES), (lambda b, s: (0, 0)) if has_state else (lambda b, s: (s, 0))),
    ]
    args = [x, mod, mod, mod, wts["win"], wts["wkt"], wts["wgt"], wts["g_pre_mix"], wts["g_post_mix"],
            wts["bg_row"], wts["ln_v_g"], wts["ln_v_b"], wts["w_spatial"], wts["b_spatial_b"],
            wts["swa_sinks"][layer], wts["w_br_mlstm"], wts["w_br_gmlp"], wts["w_br_swa"], wts["w_out"],
            cos_t, sin_t]
    n_new = Lr if has_state else WINDOW
    out_shape = [
        jax.ShapeDtypeStruct((B, S, D_MODEL), F32),
        jax.ShapeDtypeStruct((B, ML_HEADS, ML_DH, 2 * ML_DH), F32),
        jax.ShapeDtypeStruct((B, ML_HEADS, LANES), F32),
        jax.ShapeDtypeStruct((B, n_new, SW_KVW), F32),
        jax.ShapeDtypeStruct((B, n_new, SW_KVW), F32),
    ]
    out_specs = [
        pl.BlockSpec((R, Lr, D_MODEL), lambda b, s: (b, s, 0)),
        pl.BlockSpec((R, ML_HEADS, ML_DH, 2 * ML_DH), lambda b, s: (b, 0, 0, 0)),
        pl.BlockSpec((R, ML_HEADS, LANES), lambda b, s: (b, 0, 0)),
        pl.BlockSpec((R, n_new, SW_KVW), lambda b, s: (b, 0, 0)),
        pl.BlockSpec((R, n_new, SW_KVW), lambda b, s: (b, 0, 0)),
    ]
    scratch = [pltpu.VMEM((T, ML_W), BF16), pltpu.VMEM((T, SG_W), BF16), pltpu.VMEM((T, SW_W), BF16)]
    if has_state:
        caug_in, m_in, cache_k, cache_v = state
        in_specs += [
            pl.BlockSpec((R, ML_HEADS, ML_DH, 2 * ML_DH), lambda b, s: (b, 0, 0, 0)),
            pl.BlockSpec((R, ML_HEADS, LANES), lambda b, s: (b, 0, 0)),
            pl.BlockSpec((R, WINDOW, SW_KVW), lambda b, s: (b, 0, 0)),
            pl.BlockSpec((R, WINDOW, SW_KVW), lambda b, s: (b, 0, 0)),
        ]
        args += [caug_in, m_in, cache_k, cache_v]
        out_shape.append(jax.ShapeDtypeStruct((B, S, SG_W), F32))
        out_specs.append(pl.BlockSpec((R, Lr, SG_W), lambda b, s: (b, s, 0)))
    else:
        scratch += [pltpu.VMEM((WINDOW, SW_KVW), F32), pltpu.VMEM((WINDOW, SW_KVW), F32)]

    return pl.pallas_call(
        functools.partial(_mixer_kernel, R=R, Lr=Lr, Lg=Lg, has_state=has_state, n_seq=n_seq),
        grid=(B // R, n_seq),
        in_specs=in_specs,
        out_specs=out_specs,
        out_shape=out_shape,
        scratch_shapes=scratch,
        compiler_params=pltpu.CompilerParams(
            dimension_semantics=("parallel", "arbitrary"), vmem_limit_bytes=VMEM_LIMIT),
        name="mixer_sample" if has_state else "mixer_prompt",
    )(*args)


def _ffn_kernel(x_ref, sh_ref, sc_ref, g_ref, gpre_ref, gpost_ref, wfi_ref, wfo_ref, o_ref, *, R, Lr):
    T = R * Lr
    x = x_ref[...].reshape(T, D_MODEL)
    h = _rms(x) * gpre_ref[...] * (1.0 + _rows(sc_ref[...], R, Lr)) + _rows(sh_ref[...], R, Lr)
    hb = h.astype(BF16)
    f = None
    for lo, hi in FFN_SPLITS:
        gate = _dot(hb, wfi_ref[:, lo:hi])
        up = _dot(hb, wfi_ref[:, FFN_HIDDEN + lo:FFN_HIDDEN + hi])
        act = (gate * _sigmoid(gate) * up).astype(BF16)
        part = _dot(act, wfo_ref[lo:hi, :])
        f = part if f is None else f + part
    y = x + _rows(g_ref[...], R, Lr) * (_rms(f) * gpost_ref[...])
    o_ref[...] = y.reshape(R, Lr, D_MODEL)


def _ffn(x, mod, b_off, layer, wts, *, R, Lr):
    B, S, _ = x.shape

    def mod_spec(j):
        return pl.BlockSpec((None, None, R, 1, D_MODEL), lambda b, s: (layer, j, b + b_off // R, 0, 0))

    return pl.pallas_call(
        functools.partial(_ffn_kernel, R=R, Lr=Lr),
        grid=(B // R, S // Lr),
        in_specs=[
            pl.BlockSpec((R, Lr, D_MODEL), lambda b, s: (b, s, 0)),
            mod_spec(3), mod_spec(4), mod_spec(5),
            _const_spec((1, D_MODEL), layer),
            _const_spec((1, D_MODEL), layer),
            _const_spec((D_MODEL, 2 * FFN_HIDDEN), layer),
            _const_spec((FFN_HIDDEN, D_MODEL), layer),
        ],
        out_specs=pl.BlockSpec((R, Lr, D_MODEL), lambda b, s: (b, s, 0)),
        out_shape=jax.ShapeDtypeStruct((B, S, D_MODEL), F32),
        compiler_params=pltpu.CompilerParams(
            dimension_semantics=("parallel", "parallel"), vmem_limit_bytes=VMEM_LIMIT),
        name="ffn",
    )(x, mod, mod, mod, wts["g_pre_ffn"], wts["g_post_ffn"], wts["w_ffn_in"], wts["w_ffn_out"])


_PAIR_PERM = np.concatenate([np.arange(0, 32), np.arange(64, 96), np.arange(32, 64), np.arange(96, 128)])
_Q_PERM = np.concatenate([
    np.concatenate([np.arange(i * SW_DH, (i + 1) * SW_DH),
                    np.arange((SW_REP + i) * SW_DH, (SW_REP + i + 1) * SW_DH)])[_PAIR_PERM]
    for i in range(SW_REP)])
_O_PERM = np.concatenate([
    np.concatenate([np.arange(i * SW_DH, (i + 1) * SW_DH),
                    np.arange((SW_REP + i) * SW_DH, (SW_REP + i + 1) * SW_DH)])
    for i in range(SW_REP)])
_PAIR_INV = np.argsort(_PAIR_PERM)


def _rope_tables(pos):
    inv = ROPE_THETA ** (-jnp.arange(SW_HALF, dtype=F32) / SW_HALF)
    ang = pos.astype(F32)[:, None] * inv[None, :]
    cos, sin = jnp.cos(ang), jnp.sin(ang)
    cos_t = jnp.concatenate([cos, cos, cos, cos], axis=1)
    sin_t = jnp.concatenate([-sin, -sin, sin, sin], axis=1)
    return cos_t, sin_t


def _prep_weights(w_in, b_igate, b_fgate, g_pre_mix, g_post_mix, g_pre_ffn, g_post_ffn, ln_v_g, ln_v_b,
                  w_spatial, b_spatial, swa_sinks, w_br_mlstm, w_br_gmlp, w_br_swa, w_out, w_ffn_in, w_ffn_out):
    offs = np.cumsum((0,) + SPLIT_SIZES)
    seg = [w_in[..., offs[i]:offs[i + 1]] for i in range(len(SPLIT_SIZES))]
    mq, mk, mv, mo, mi, mf, su, sv, wq, wk, wv, gl = seg
    gates = jnp.concatenate([mi, mf], axis=-1)
    win = jnp.concatenate([mq, mv, mo, su, sv, wq[..., _Q_PERM], wk[..., _PAIR_PERM], wv, gl],
                          axis=-1).astype(BF16)
    bg = jnp.concatenate([b_igate, b_fgate], axis=-1)
    row = lambda a: a.reshape(DEPTH, 1, a.shape[-1])
    return dict(
        win=win,
        wkt=jnp.swapaxes(mk, 1, 2).astype(BF16),
        wgt=jnp.pad(jnp.swapaxes(gates, 1, 2), ((0, 0), (0, GATE_ROWS - N_GATE), (0, 0))).astype(BF16),
        bg_row=jnp.pad(bg, ((0, 0), (0, GATE_ROWS - N_GATE))).reshape(DEPTH, GATE_ROWS, 1),
        g_pre_mix=row(g_pre_mix), g_post_mix=row(g_post_mix),
        g_pre_ffn=row(g_pre_ffn), g_post_ffn=row(g_post_ffn),
        ln_v_g=row(ln_v_g), ln_v_b=row(ln_v_b),
        w_spatial=w_spatial,
        b_spatial_b=jnp.broadcast_to(b_spatial[..., None], b_spatial.shape + (SG_GDIM,)),
        swa_sinks=swa_sinks,
        w_br_mlstm=w_br_mlstm.astype(BF16), w_br_gmlp=w_br_gmlp.astype(BF16),
        w_br_swa=w_br_swa[:, _O_PERM, :].astype(BF16), w_out=w_out.astype(BF16),
        w_ffn_in=w_ffn_in.astype(BF16), w_ffn_out=w_ffn_out.astype(BF16),
    )


MIX_TOKENS_PROMPT = 256
MIX_ROWS_SAMPLE = 4
FFN_TOKENS = 512


def kernel(x_prompt, x_sample, c_prompt, c_sample, state_mlstm_C, state_mlstm_n, state_mlstm_m, cache_swa_k, cache_swa_v, w_ada, b_ada, g_pre_mix, g_post_mix, g_pre_ffn, g_post_ffn, w_in, b_igate, b_fgate, ln_v_g, ln_v_b, w_spatial, b_spatial, swa_sinks, w_br_mlstm, w_br_gmlp, w_br_swa, w_out, w_ffn_in, w_ffn_out):
    Bp, Sp, _ = x_prompt.shape
    Bs, Ss, _ = x_sample.shape
    wts = _prep_weights(w_in, b_igate, b_fgate, g_pre_mix, g_post_mix, g_pre_ffn, g_post_ffn, ln_v_g, ln_v_b,
                        w_spatial, b_spatial, swa_sinks, w_br_mlstm, w_br_gmlp, w_br_swa, w_out,
                        w_ffn_in, w_ffn_out)
    mod = _modulation(jnp.concatenate([c_prompt, c_sample], axis=0), w_ada, b_ada)
    mod = mod.reshape(DEPTH, 6, Bp + Bs, 1, D_MODEL)
    cos_p, sin_p = _rope_tables(jnp.arange(Sp))
    cos_s, sin_s = _rope_tables(PAST_LEN + jnp.arange(Ss))

    n_rep = jnp.broadcast_to(state_mlstm_n[..., None], state_mlstm_n.shape + (ML_DH,))
    caug_in = jnp.concatenate([state_mlstm_C, n_rep], axis=-1)
    m_in = jnp.broadcast_to(state_mlstm_m[..., None], state_mlstm_m.shape + (LANES,))
    ck = cache_swa_k.reshape(DEPTH, Bs, WINDOW, SW_KVW)[..., _PAIR_PERM]
    cv = cache_swa_v.reshape(DEPTH, Bs, WINDOW, SW_KVW)

    xp, xs = x_prompt, x_sample
    outs_p, outs_s = [], []
    for l in range(DEPTH):
        xp, caug_p, m_p, k_p, v_p = _mixer(xp, mod, 0, l, wts, cos_p, sin_p, None,
                                           R=1, Lr=MIX_TOKENS_PROMPT, Lg=SG_CHUNK)
        xp = _ffn(xp, mod, 0, l, wts, R=1, Lr=FFN_TOKENS)
        xs, caug_s, m_s, k_s, v_s, gv_s = _mixer(xs, mod, Bp, l, wts, cos_s, sin_s,
                                                 (caug_in[l], m_in[l], ck[l], cv[l]),
                                                 R=MIX_ROWS_SAMPLE, Lr=Ss, Lg=Ss)
        xs = _ffn(xs, mod, Bp, l, wts, R=FFN_TOKENS // Ss, Lr=Ss)
        outs_p.append((caug_p, m_p, k_p, v_p))
        outs_s.append((caug_s, m_s, k_s, v_s, gv_s))

    def unpack(outs, B, n_new):
        caug = jnp.stack([o[0] for o in outs])
        C = caug[..., :ML_DH]
        n = caug[..., ML_DH]
        m = jnp.stack([o[1] for o in outs])[..., 0]
        k = jnp.stack([o[2] for o in outs])[..., _PAIR_INV].reshape(DEPTH, B, n_new, SW_KV, SW_DH)
        v = jnp.stack([o[3] for o in outs]).reshape(DEPTH, B, n_new, SW_KV, SW_DH)
        return C, n, m, k, v

    C_p, n_p, m_p, k_p, v_p = unpack(outs_p, Bp, WINDOW)
    C_s, n_s, m_s, k_s, v_s = unpack(outs_s, Bs, Ss)
    gv_s = jnp.stack([o[4] for o in outs_s])
    return (xp, xs, C_p, n_p, m_p, k_p, v_p, C_s, n_s, m_s, k_s, v_s, gv_s)
```

```python
import functools

import numpy as np
import jax
import jax.numpy as jnp
from jax import lax
from jax.experimental import pallas as pl
from jax.experimental.pallas import tpu as pltpu

F32 = jnp.float32
BF16 = jnp.bfloat16

D_MODEL = 1024
DEPTH = 4
PAST_LEN = 1024
ML_HEADS = 4
ML_DH = 128
ML_W = ML_HEADS * ML_DH
SG_GROUPS = 4
SG_CHUNK = 128
SG_W = 512
SG_GDIM = SG_W // SG_GROUPS
SW_HEADS = 8
SW_KV = 2
SW_REP = SW_HEADS // SW_KV
SW_DH = 64
SW_HALF = SW_DH // 2
SW_W = SW_HEADS * SW_DH
SW_KVW = SW_KV * SW_DH
SW_CHUNK = 64
WINDOW = 128
BAND = WINDOW + SW_CHUNK
ROPE_THETA = 10000.0
FFN_HIDDEN = 2816
EPS = 1e-6
SPLIT_SIZES = (ML_W, ML_W, ML_W, ML_W, ML_HEADS, ML_HEADS, SG_W, SG_W, SW_W, SW_KVW, SW_KVW, 3 * D_MODEL)

LANES = 128
N_GATE = 2 * ML_HEADS
GATE_ROWS = 16

OFF_MQ = 0
OFF_MV = OFF_MQ + ML_W
OFF_MO = OFF_MV + ML_W
OFF_SU = OFF_MO + ML_W
OFF_SV = OFF_SU + SG_W
OFF_WQ = OFF_SV + SG_W
OFF_WK = OFF_WQ + SW_W
OFF_WV = OFF_WK + SW_KVW
OFF_GL = OFF_WV + SW_KVW
D_IN_PAD = OFF_GL + 3 * D_MODEL

FFN_SPLITS = ((0, 1536), (1536, FFN_HIDDEN))

VMEM_LIMIT = 56 * 1024 * 1024

NT_DIMS = (((1,), (1,)), ((), ()))


def _dot(a, b):
    return jnp.dot(a, b, preferred_element_type=F32)


def _dot_nt(a, b):
    return lax.dot_general(a, b, NT_DIMS, preferred_element_type=F32)


def _sigmoid(x):
    return 0.5 * jnp.tanh(0.5 * x) + 0.5


def _rms(x):
    return x * lax.rsqrt(jnp.mean(x * x, axis=-1, keepdims=True) + EPS)


def _rows(m3, R, Lr):
    W = m3.shape[-1]
    if R == 1:
        return m3.reshape(1, W)
    return jnp.broadcast_to(m3, (R, Lr, W)).reshape(R * Lr, W)


def _split3(x):
    h1 = x.astype(BF16)
    r1 = x - h1.astype(F32)
    h2 = r1.astype(BF16)
    h3 = (r1 - h2.astype(F32)).astype(BF16)
    return h1, h2, h3


class _Fillers:
    def __init__(self, total_weight):
        self._thunks, self._vals, self._left = {}, {}, total_weight

    def add(self, name, thunk):
        self._thunks[name] = thunk

    def get(self, name):
        if name not in self._vals:
            self._vals[name] = self._thunks.pop(name)()
        return self._vals[name]

    def pump(self, weight):
        n = -(-len(self._thunks) * weight // max(self._left, 1))
        self._left -= weight
        for name in list(self._thunks)[:n]:
            self.get(name)


def _mod_kernel(c_ref, w_ref, b_ref, o_ref):
    c = c_ref[...]
    a = (c * _sigmoid(c)).astype(BF16)
    o_ref[...] = _dot(a, w_ref[...].astype(BF16)) + b_ref[...]


def _modulation(c_all, w_ada, b_ada):
    nb = c_all.shape[0]
    return pl.pallas_call(
        _mod_kernel,
        grid=(DEPTH, 6),
        in_specs=[
            pl.BlockSpec((nb, D_MODEL), lambda l, j: (0, 0)),
            pl.BlockSpec((None, D_MODEL, D_MODEL), lambda l, j: (l, 0, j)),
            pl.BlockSpec((None, None, 1, D_MODEL), lambda l, j: (l, j, 0, 0)),
        ],
        out_specs=pl.BlockSpec((None, None, nb, D_MODEL), lambda l, j: (l, j, 0, 0)),
        out_shape=jax.ShapeDtypeStruct((DEPTH, 6, nb, D_MODEL), F32),
        compiler_params=pltpu.CompilerParams(dimension_semantics=("parallel", "parallel")),
        name="adaln_modulation",
    )(c_all, w_ada, b_ada.reshape(DEPTH, 6, 1, D_MODEL))


def _rope(x, cos, sin_signed):
    T, W = x.shape
    outs = []
    for j in range(W // LANES):
        xb = x[:, j * LANES:(j + 1) * LANES]
        outs.append(xb * cos + pltpu.roll(xb, LANES // 2, 1) * sin_signed)
    return outs[0] if len(outs) == 1 else jnp.concatenate(outs, axis=1)


def _mlstm_unit(qb, kt, v, a_row, b_col, caug, m_prev):
    L = qb.shape[0]
    ti = lax.broadcasted_iota(jnp.int32, (L, L), 0)
    si = lax.broadcasted_iota(jnp.int32, (L, L), 1)
    a_tri = jnp.where(ti >= si, a_row, -jnp.inf)
    g = jnp.maximum(jnp.max(a_tri, axis=-1, keepdims=True), m_prev)
    w_intra = jnp.exp(a_tri - g)
    w_inter = jnp.exp(m_prev - g)
    s = _dot(qb, kt.astype(BF16)) * w_intra
    vaug = jnp.concatenate([v, jnp.ones_like(v)], axis=1).astype(BF16)
    qc = _dot(qb, caug.astype(BF16))
    sv = _dot(s.astype(BF16), vaug)
    num = w_inter * qc[:, :ML_DH] + sv[:, :ML_DH]
    den = w_inter * qc[:, ML_DH:] + sv[:, ML_DH:]
    floor = jnp.exp(-(jnp.broadcast_to(b_col, (L, ML_DH)) + g))
    h = num / jnp.maximum(jnp.abs(den), floor)
    g_last = g[L - 1:L, :]
    ws_row = jnp.exp(a_row - g_last)
    upd = _dot((kt * ws_row).astype(BF16), vaug)
    caug_new = jnp.exp(m_prev - g_last) * caug + upd
    return h, caug_new, b_col[L - 1:L, :] + g_last


def _swa_chunk(qstack, kg, vaug, sink_cols, mask):
    o = None
    e_cols = []
    for g in range(SW_KV):
        s = _dot_nt(qstack, kg[g])
        if mask is not None:
            s = jnp.where(mask, s, -jnp.inf)
        mx = jnp.maximum(jnp.max(s, axis=-1, keepdims=True), sink_cols[g])
        p = jnp.exp(s - mx).astype(BF16)
        e_cols.append(jnp.exp(sink_cols[g] - mx))
        og = _dot(p, vaug[g])
        o = og if o is None else o + og
    lane = lax.broadcasted_iota(jnp.int32, (qstack.shape[0], LANES), 1)
    e = jnp.where(lane < SW_DH, e_cols[0], e_cols[1])
    return o[:, :LANES] / (o[:, LANES:] + e)


def _mixer_kernel(*refs, R, Lr, Lg, has_state, n_seq, n_alias):
    it = iter(refs)
    x_ref, sh1_ref, sc1_ref, g1_ref = next(it), next(it), next(it), next(it)
    win_ref, wkt_ref, wgt_ref, gpre_ref, gpost_ref = next(it), next(it), next(it), next(it), next(it)
    bgr_ref = next(it)
    lng_ref, lnb_ref, wsp_ref, bsp_ref = next(it), next(it), next(it), next(it)
    sinks_ref = next(it)
    wbm_ref, wbg_ref, wbs_ref, wout_ref = next(it), next(it), next(it), next(it)
    cos_ref, sin_ref = next(it), next(it)
    if has_state:
        cin_ref, nin_ref, min_ref, ck_ref, cv_ref = next(it), next(it), next(it), next(it), next(it)
    for _ in range(n_alias):
        next(it)
    xo_ref, cout_ref, nout_ref, m_ref, kn_ref, vn_ref = (next(it), next(it), next(it), next(it), next(it),
                                                         next(it))
    if has_state:
        gv_ref = next(it)
    st_ref, yml_ref, ysg_ref, ysw_ref = next(it), next(it), next(it), next(it)
    if not has_state:
        kprev_ref, vprev_ref = next(it), next(it)

    T = R * Lr
    seq = pl.program_id(1)

    @pl.when(seq == 0)
    def _init():
        if has_state:
            for r in range(R):
                for hd in range(ML_HEADS):
                    st_ref[r, hd, :, :ML_DH] = cin_ref[r, hd]
                    st_ref[r, hd, :, ML_DH:] = jnp.broadcast_to(nin_ref[r, hd:hd + 1, :], (ML_DH, ML_DH)).T
            m_ref[...] = min_ref[...]
        else:
            st_ref[...] = jnp.zeros_like(st_ref)
            m_ref[...] = jnp.zeros_like(m_ref)
            kprev_ref[...] = jnp.zeros_like(kprev_ref)
            vprev_ref[...] = jnp.zeros_like(vprev_ref)

    x = x_ref[...].reshape(T, D_MODEL)
    h = _rms(x) * gpre_ref[...] * (1.0 + _rows(sc1_ref[...], R, Lr)) + _rows(sh1_ref[...], R, Lr)
    hb = h.astype(BF16)

    def proj(lo, width):
        return _dot(hb, win_ref[:, lo:lo + width])

    n_sw = Lr // SW_CHUNK
    w_ml, w_sg, w_sw = (3, 1, 3) if Lr > SW_CHUNK else (1, 1, 1)
    fill = _Fillers(R * ML_HEADS * w_ml + SG_GROUPS * w_sg + R * n_sw * w_sw)

    def gmlp_inputs():
        sv = proj(OFF_SV, SG_W)
        mu = jnp.mean(sv, axis=-1, keepdims=True)
        svc = sv - mu
        var = jnp.mean(svc * svc, axis=-1, keepdims=True)
        return svc * lax.rsqrt(var + EPS) * lng_ref[...] + lnb_ref[...]

    def rope_tables():
        cos, sin = cos_ref[...], sin_ref[...]
        if R > 1:
            cos = jnp.concatenate([cos] * R, axis=0)
            sin = jnp.concatenate([sin] * R, axis=0)
        return cos, sin

    fill.add("mo", lambda: _sigmoid(proj(OFF_MO, ML_W)))
    fill.add("su", lambda: proj(OFF_SU, SG_W))
    fill.add("vnorm", gmlp_inputs)
    fill.add("wq", lambda: (_rope(proj(OFF_WQ, SW_W), *rope_tables()) * (SW_DH ** -0.5)).astype(BF16))
    fill.add("wk", lambda: _rope(proj(OFF_WK, SW_KVW), *rope_tables()))
    fill.add("wv", lambda: proj(OFF_WV, SW_KVW))
    half = D_MODEL // 2
    for i in range(6):
        fill.add(("gl", i), functools.partial(lambda i: _sigmoid(proj(OFF_GL + i * half, half)), i))

    grow = _dot_nt(wgt_ref[...], hb) + bgr_ref[...]
    lf_row = jax.nn.log_sigmoid(grow)
    ri = lax.broadcasted_iota(jnp.int32, (Lr, Lr), 0)
    ci = lax.broadcasted_iota(jnp.int32, (Lr, Lr), 1)
    triu = jnp.where(ri <= ci, 1.0, 0.0).astype(BF16)
    triu3 = jnp.concatenate([triu] * 3, axis=0)
    mqb = proj(OFF_MQ, ML_W).astype(BF16)
    mkt = _dot_nt(wkt_ref[...], hb) * (ML_DH ** -0.5)
    mv = proj(OFF_MV, ML_W)
    for r in range(R):
        rs = slice(r * Lr, (r + 1) * Lr)
        brow_all = _dot(jnp.concatenate(_split3(lf_row[:, rs]), axis=1), triu3)
        bcol_all = brow_all.T
        for hd in range(ML_HEADS):
            hs = slice(hd * ML_DH, (hd + 1) * ML_DH)
            fg = ML_HEADS + hd
            h_ml, caug_new, m_new = _mlstm_unit(
                mqb[rs, hs], mkt[hs, rs], mv[rs, hs],
                grow[hd:hd + 1, rs] - brow_all[fg:fg + 1, :], bcol_all[:, fg:fg + 1],
                st_ref[r, hd], m_ref[r, hd:hd + 1, 0:1])
            st_ref[r, hd] = caug_new
            m_ref[r, hd:hd + 1, :] = jnp.broadcast_to(m_new, (1, LANES))
            fill.pump(w_ml)
            yml_ref[rs, hs] = (fill.get("mo")[rs, hs] * h_ml).astype(BF16)
    fill.add("ml", lambda: _dot(yml_ref[...], wbm_ref[...]))

    vnorm = fill.get("vnorm")
    su = fill.get("su")
    if has_state:
        gv_ref[...] = vnorm.reshape(R, Lr, SG_W)
    n_chunk = T // Lg
    gi = lax.broadcasted_iota(jnp.int32, (Lg, Lg), 0)
    gj = lax.broadcasted_iota(jnp.int32, (Lg, Lg), 1)
    for g in range(SG_GROUPS):
        cs = slice(g * SG_GDIM, (g + 1) * SG_GDIM)
        w = jnp.where(gi >= gj, wsp_ref[g, :Lg, :Lg], 0.0).astype(BF16)
        vcat = jnp.concatenate([vnorm[c * Lg:(c + 1) * Lg, cs] for c in range(n_chunk)], axis=1)
        z = _dot(w, vcat.astype(BF16))
        for c in range(n_chunk):
            ts = slice(c * Lg, (c + 1) * Lg)
            zc = z[:, c * SG_GDIM:(c + 1) * SG_GDIM] + bsp_ref[g, :Lg, :]
            ysg_ref[ts, cs] = (su[ts, cs] * zc).astype(BF16)
        fill.pump(w_sg)
    fill.add("sg", lambda: _dot(ysg_ref[...], wbg_ref[...]))

    wqb, wk, wv = fill.get("wq"), fill.get("wk"), fill.get("wv")
    n_q = SW_REP * SW_CHUNK
    rid = lax.broadcasted_iota(jnp.int32, (n_q, 1), 0) // SW_CHUNK
    sink_cols = []
    for g in range(SW_KV):
        col = jnp.full((n_q, 1), sinks_ref[SW_REP * g + SW_REP - 1], F32)
        for i in range(SW_REP - 2, -1, -1):
            col = jnp.where(rid == i, sinks_ref[SW_REP * g + i], col)
        sink_cols.append(col)
    kpos_chunk = lax.broadcasted_iota(jnp.int32, (n_q, BAND), 1) // SW_CHUNK
    for r in range(R):
        if has_state:
            kband = jnp.concatenate([ck_ref[r], wk[r * Lr:(r + 1) * Lr]], axis=0)
            vband = jnp.concatenate([cv_ref[r], wv[r * Lr:(r + 1) * Lr]], axis=0)
        else:
            kband = jnp.concatenate([kprev_ref[...], wk], axis=0)
            vband = jnp.concatenate([vprev_ref[...], wv], axis=0)
        lane = lax.broadcasted_iota(jnp.int32, kband.shape, 1)
        k_group = (lane // SW_HALF) % SW_KV
        v_group = lane // SW_DH
        kg = [jnp.where(k_group == g, kband, 0.0).astype(BF16) for g in range(SW_KV)]
        vaug = [jnp.concatenate([jnp.where(v_group == g, vband, 0.0),
                                 jnp.where(v_group == g, 1.0, 0.0)], axis=1).astype(BF16)
                for g in range(SW_KV)]
        for j in range(n_sw):
            t0 = r * Lr + j * SW_CHUNK
            qstack = jnp.concatenate([wqb[t0:t0 + SW_CHUNK, i * LANES:(i + 1) * LANES]
                                      for i in range(SW_REP)], axis=0)
            needs_mask = not has_state and j < WINDOW // SW_CHUNK
            mask = kpos_chunk >= (WINDOW // SW_CHUNK) - (seq * n_sw + j) if needs_mask else None
            ks = slice(j * SW_CHUNK, j * SW_CHUNK + BAND)
            y = _swa_chunk(qstack, [k[ks] for k in kg], [v[ks] for v in vaug], sink_cols, mask)
            for i in range(SW_REP):
                ysw_ref[t0:t0 + SW_CHUNK, i * LANES:(i + 1) * LANES] = (
                    y[i * SW_CHUNK:(i + 1) * SW_CHUNK].astype(BF16))
            fill.pump(w_sw)
    if has_state:
        kn_ref[...] = wk.reshape(R, Lr, SW_KVW)
        vn_ref[...] = wv.reshape(R, Lr, SW_KVW)
    else:
        kprev_ref[...] = wk[T - WINDOW:]
        vprev_ref[...] = wv[T - WINDOW:]

    @pl.when(seq == n_seq - 1)
    def _emit_state():
        if not has_state:
            kn_ref[0] = wk[T - WINDOW:]
            vn_ref[0] = wv[T - WINDOW:]
        for r in range(R):
            for hd in range(ML_HEADS):
                cout_ref[r, hd] = st_ref[r, hd, :, :ML_DH]
                nout_ref[r, hd:hd + 1, :] = st_ref[r, hd, :, ML_DH:].T[0:1, :]

    gate = lambda b: jnp.concatenate([fill.get(("gl", 2 * b)), fill.get(("gl", 2 * b + 1))], axis=1)
    merged = gate(0) * fill.get("ml")
    merged = merged + gate(1) * fill.get("sg")
    merged = merged + gate(2) * _dot(ysw_ref[...], wbs_ref[...])
    o = _dot(merged.astype(BF16), wout_ref[...])
    y = x + _rows(g1_ref[...], R, Lr) * (_rms(o) * gpost_ref[...])
    xo_ref[...] = y.reshape(R, Lr, D_MODEL)


def _const_spec(shape, layer):
    nd = len(shape)
    return pl.BlockSpec((None,) + tuple(shape), lambda b, s: (layer,) + (0,) * nd,
                        pipeline_mode=pl.Buffered(1))


def _mixer(x, mod, layer, wts, cos_t, sin_t, state, prev, *, R, Lr, Lg):
    B, S, _ = x.shape
    n_seq = S // Lr
    has_state = state is not None
    T = R * Lr

    def mod_spec(j):
        return pl.BlockSpec((None, None, R, 1, D_MODEL), lambda b, s: (layer, j, b, 0, 0))

    in_specs = [
        pl.BlockSpec((R, Lr, D_MODEL), lambda b, s: (b, s, 0)),
        mod_spec(0), mod_spec(1), mod_spec(2),
        _const_spec((D_MODEL, D_IN_PAD), layer),
        _const_spec((ML_W, D_MODEL), layer),
        _const_spec((GATE_ROWS, D_MODEL), layer),
        _const_spec((1, D_MODEL), layer),
        _const_spec((1, D_MODEL), layer),
        _const_spec((GATE_ROWS, 1), layer),
        _const_spec((1, SG_W), layer),
        _const_spec((1, SG_W), layer),
        _const_spec((SG_GROUPS, SG_CHUNK, SG_CHUNK), layer),
        _const_spec((SG_GROUPS, SG_CHUNK, SG_GDIM), layer),
        pl.BlockSpec(memory_space=pltpu.SMEM),
        _const_spec((ML_W, D_MODEL), layer),
        _const_spec((SG_W, D_MODEL), layer),
        _const_spec((SW_W, D_MODEL), layer),
        _const_spec((D_MODEL, D_MODEL), layer),
        pl.BlockSpec((Lr, LANES), (lambda b, s: (0, 0)) if has_state else (lambda b, s: (s, 0))),
        pl.BlockSpec((Lr, LANES), (lambda b, s: (0, 0)) if has_state else (lambda b, s: (s, 0))),
    ]
    args = [x, mod, mod, mod, wts["win"], wts["wkt"], wts["wgt"], wts["g_pre_mix"], wts["g_post_mix"],
            wts["bg_row"], wts["ln_v_g"], wts["ln_v_b"], wts["w_spatial"], wts["b_spatial_b"],
            wts["swa_sinks"][layer], wts["w_br_mlstm"], wts["w_br_gmlp"], wts["w_br_swa"], wts["w_out"],
            cos_t, sin_t]
    n_new = Lr if has_state else WINDOW
    stacked = [
        ((DEPTH, B, ML_HEADS, ML_DH, ML_DH), (None, R, ML_HEADS, ML_DH, ML_DH), lambda b, s: (layer, b, 0, 0, 0)),
        ((DEPTH, B, ML_HEADS, ML_DH), (None, R, ML_HEADS, ML_DH), lambda b, s: (layer, b, 0, 0)),
        ((DEPTH, B, n_new, SW_KVW), (None, R, n_new, SW_KVW), lambda b, s: (layer, b, 0, 0)),
        ((DEPTH, B, n_new, SW_KVW), (None, R, n_new, SW_KVW), lambda b, s: (layer, b, 0, 0)),
    ]
    if has_state:
        stacked.append(((DEPTH, B, S, SG_W), (None, R, Lr, SG_W), lambda b, s: (layer, b, s, 0)))
    st_shapes = [jax.ShapeDtypeStruct(shp, F32) for shp, _, _ in stacked]
    st_specs = [pl.BlockSpec(blk, imap) for _, blk, imap in stacked]
    x_out = (jax.ShapeDtypeStruct((B, S, D_MODEL), F32), pl.BlockSpec((R, Lr, D_MODEL), lambda b, s: (b, s, 0)))
    m_out = (jax.ShapeDtypeStruct((B, ML_HEADS, LANES), F32),
             pl.BlockSpec((R, ML_HEADS, LANES), lambda b, s: (b, 0, 0)))
    out_shape = [x_out[0], st_shapes[0], st_shapes[1], m_out[0]] + st_shapes[2:]
    out_specs = [x_out[1], st_specs[0], st_specs[1], m_out[1]] + st_specs[2:]
    stacked_out_idx = [1, 2] + list(range(4, 4 + len(stacked) - 2))
    scratch = [pltpu.VMEM((R, ML_HEADS, ML_DH, 2 * ML_DH), F32),
               pltpu.VMEM((T, ML_W), BF16), pltpu.VMEM((T, SG_W), BF16), pltpu.VMEM((T, SW_W), BF16)]
    if has_state:
        c_in, n_in, m_in, cache_k, cache_v = state
        in_specs += [
            pl.BlockSpec((None, R, ML_HEADS, ML_DH, ML_DH), lambda b, s: (layer, b, 0, 0, 0)),
            pl.BlockSpec((None, R, ML_HEADS, ML_DH), lambda b, s: (layer, b, 0, 0)),
            pl.BlockSpec((None, R, ML_HEADS, LANES), lambda b, s: (layer, b, 0, 0)),
            pl.BlockSpec((None, R, WINDOW, SW_KVW), lambda b, s: (layer, b, 0, 0)),
            pl.BlockSpec((None, R, WINDOW, SW_KVW), lambda b, s: (layer, b, 0, 0)),
        ]
        args += [c_in, n_in, m_in, cache_k, cache_v]
    else:
        scratch += [pltpu.VMEM((WINDOW, SW_KVW), F32), pltpu.VMEM((WINDOW, SW_KVW), F32)]
    aliases = {}
    if prev is not None:
        for arr, oi in zip(prev, stacked_out_idx):
            aliases[len(args)] = oi
            in_specs.append(pl.BlockSpec(memory_space=pl.ANY))
            args.append(arr)

    outs = pl.pallas_call(
        functools.partial(_mixer_kernel, R=R, Lr=Lr, Lg=Lg, has_state=has_state, n_seq=n_seq,
                          n_alias=len(aliases)),
        grid=(B // R, n_seq),
        in_specs=in_specs,
        out_specs=out_specs,
        out_shape=out_shape,
        scratch_shapes=scratch,
        input_output_aliases=aliases,
        compiler_params=pltpu.CompilerParams(
            dimension_semantics=("parallel", "arbitrary"), vmem_limit_bytes=VMEM_LIMIT),
        name="mixer_sample" if has_state else "mixer_prompt",
    )(*args)
    return outs[0], outs[3], [outs[i] for i in stacked_out_idx]


def _ffn_kernel(x_ref, sh_ref, sc_ref, g_ref, gpre_ref, gpost_ref, wfi_ref, wfo_ref, o_ref, *, R, Lr, n_sub):
    Rs = R // n_sub if R > 1 else 1
    Ls = Lr if R > 1 else Lr // n_sub
    Ts = Rs * Ls

    def rows_of(ref, k):
        return _rows(ref[k * Rs:(k + 1) * Rs] if R > 1 else ref[...], Rs, Ls)

    def load(k):
        xk = x_ref[k * Rs:(k + 1) * Rs] if R > 1 else x_ref[:, k * Ls:(k + 1) * Ls, :]
        return xk.reshape(Ts, D_MODEL)

    def prologue(k):
        x = load(k)
        h = _rms(x) * gpre_ref[...] * (1.0 + rows_of(sc_ref, k)) + rows_of(sh_ref, k)
        return h.astype(BF16)

    def matmuls(hb):
        f = None
        for lo, hi in FFN_SPLITS:
            gate = _dot(hb, wfi_ref[:, lo:hi])
            up = _dot(hb, wfi_ref[:, FFN_HIDDEN + lo:FFN_HIDDEN + hi])
            act = (gate * _sigmoid(gate) * up).astype(BF16)
            part = _dot(act, wfo_ref[lo:hi, :])
            f = part if f is None else f + part
        return f

    def epilogue(k, f):
        y = (load(k) + rows_of(g_ref, k) * (_rms(f) * gpost_ref[...])).reshape(Rs, Ls, D_MODEL)
        if R > 1:
            o_ref[k * Rs:(k + 1) * Rs] = y
        else:
            o_ref[:, k * Ls:(k + 1) * Ls, :] = y

    hb = prologue(0)
    for k in range(n_sub):
        hb_next = prologue(k + 1) if k + 1 < n_sub else None
        f = matmuls(hb)
        epilogue(k, f)
        hb = hb_next


def _ffn(x, mod, layer, wts, *, R, Lr, n_sub):
    B, S, _ = x.shape

    def mod_spec(j):
        return pl.BlockSpec((None, None, R, 1, D_MODEL), lambda b, s: (layer, j, b, 0, 0))

    return pl.pallas_call(
        functools.partial(_ffn_kernel, R=R, Lr=Lr, n_sub=n_sub),
        grid=(B // R, S // Lr),
        in_specs=[
            pl.BlockSpec((R, Lr, D_MODEL), lambda b, s: (b, s, 0)),
            mod_spec(3), mod_spec(4), mod_spec(5),
            _const_spec((1, D_MODEL), layer),
            _const_spec((1, D_MODEL), layer),
            _const_spec((D_MODEL, 2 * FFN_HIDDEN), layer),
            _const_spec((FFN_HIDDEN, D_MODEL), layer),
        ],
        out_specs=pl.BlockSpec((R, Lr, D_MODEL), lambda b, s: (b, s, 0)),
        out_shape=jax.ShapeDtypeStruct((B, S, D_MODEL), F32),
        compiler_params=pltpu.CompilerParams(
            dimension_semantics=("parallel", "parallel"), vmem_limit_bytes=VMEM_LIMIT),
        name="ffn",
    )(x, mod, mod, mod, wts["g_pre_ffn"], wts["g_post_ffn"], wts["w_ffn_in"], wts["w_ffn_out"])


def _pair_layout(a):
    lead = a.shape[:-1]
    return jnp.swapaxes(a.reshape(lead + (2, 2, SW_HALF)), -3, -2).reshape(lead + (2 * SW_DH,))


def _query_layout(wq):
    lead = wq.shape[:-1]
    w = wq.reshape(lead + (SW_KV, SW_REP, 2, SW_HALF))
    return jnp.moveaxis(w, -4, -2).reshape(lead + (SW_W,))


def _rope_tables(pos):
    inv = ROPE_THETA ** (-jnp.arange(SW_HALF, dtype=F32) / SW_HALF)
    ang = pos.astype(F32)[:, None] * inv[None, :]
    cos, sin = jnp.cos(ang), jnp.sin(ang)
    cos_t = jnp.concatenate([cos, cos, cos, cos], axis=1)
    sin_t = jnp.concatenate([-sin, -sin, sin, sin], axis=1)
    return cos_t, sin_t


def _prep_weights(w_in, b_igate, b_fgate, g_pre_mix, g_post_mix, g_pre_ffn, g_post_ffn, ln_v_g, ln_v_b,
                  w_spatial, b_spatial, swa_sinks, w_br_mlstm, w_br_gmlp, w_br_swa, w_out, w_ffn_in, w_ffn_out):
    offs = np.cumsum((0,) + SPLIT_SIZES)
    seg = [w_in[..., offs[i]:offs[i + 1]] for i in range(len(SPLIT_SIZES))]
    mq, mk, mv, mo, mi, mf, su, sv, wq, wk, wv, gl = seg
    gates = jnp.concatenate([mi, mf], axis=-1)
    win = jnp.concatenate([mq, mv, mo, su, sv, _query_layout(wq), _pair_layout(wk), wv, gl],
                          axis=-1).astype(BF16)
    bg = jnp.concatenate([b_igate, b_fgate], axis=-1)
    row = lambda a: a.reshape(DEPTH, 1, a.shape[-1])
    wbs = jnp.swapaxes(w_br_swa.reshape(DEPTH, SW_KV, SW_REP, SW_DH, D_MODEL), 1, 2).reshape(DEPTH, SW_W, D_MODEL)
    return dict(
        win=win,
        wkt=jnp.swapaxes(mk, 1, 2).astype(BF16),
        wgt=jnp.pad(jnp.swapaxes(gates, 1, 2), ((0, 0), (0, GATE_ROWS - N_GATE), (0, 0))).astype(BF16),
        bg_row=jnp.pad(bg, ((0, 0), (0, GATE_ROWS - N_GATE))).reshape(DEPTH, GATE_ROWS, 1),
        g_pre_mix=row(g_pre_mix), g_post_mix=row(g_post_mix),
        g_pre_ffn=row(g_pre_ffn), g_post_ffn=row(g_post_ffn),
        ln_v_g=row(ln_v_g), ln_v_b=row(ln_v_b),
        w_spatial=w_spatial,
        b_spatial_b=jnp.broadcast_to(b_spatial[..., None], b_spatial.shape + (SG_GDIM,)),
        swa_sinks=swa_sinks,
        w_br_mlstm=w_br_mlstm.astype(BF16), w_br_gmlp=w_br_gmlp.astype(BF16),
        w_br_swa=wbs.astype(BF16), w_out=w_out.astype(BF16),
        w_ffn_in=w_ffn_in.astype(BF16), w_ffn_out=w_ffn_out.astype(BF16),
    )


MIX_TOKENS_PROMPT = 256
MIX_ROWS_SAMPLE = 4
FFN_TOKENS = 1024
FFN_SUB = 2


def kernel(x_prompt, x_sample, c_prompt, c_sample, state_mlstm_C, state_mlstm_n, state_mlstm_m, cache_swa_k, cache_swa_v, w_ada, b_ada, g_pre_mix, g_post_mix, g_pre_ffn, g_post_ffn, w_in, b_igate, b_fgate, ln_v_g, ln_v_b, w_spatial, b_spatial, swa_sinks, w_br_mlstm, w_br_gmlp, w_br_swa, w_out, w_ffn_in, w_ffn_out):
    Bp, Sp, _ = x_prompt.shape
    Bs, Ss, _ = x_sample.shape
    wts = _prep_weights(w_in, b_igate, b_fgate, g_pre_mix, g_post_mix, g_pre_ffn, g_post_ffn, ln_v_g, ln_v_b,
                        w_spatial, b_spatial, swa_sinks, w_br_mlstm, w_br_gmlp, w_br_swa, w_out,
                        w_ffn_in, w_ffn_out)
    mod = _modulation(jnp.concatenate([c_prompt, c_sample], axis=0), w_ada, b_ada)
    mod = mod.reshape(DEPTH, 6, Bp + Bs, 1, D_MODEL)
    mod_p, mod_s = mod[:, :, :Bp], mod[:, :, Bp:]
    cos_p, sin_p = _rope_tables(jnp.arange(Sp))
    cos_s, sin_s = _rope_tables(PAST_LEN + jnp.arange(Ss))

    m_in = jnp.broadcast_to(state_mlstm_m[..., None], state_mlstm_m.shape + (LANES,))
    ck = _pair_layout(cache_swa_k.reshape(DEPTH, Bs, WINDOW, SW_KVW))
    cv = cache_swa_v.reshape(DEPTH, Bs, WINDOW, SW_KVW)
    state = (state_mlstm_C, state_mlstm_n, m_in, ck, cv)

    xp, xs = x_prompt, x_sample
    st_p, st_s, ms_p, ms_s = None, None, [], []
    for l in range(DEPTH):
        xp, m_p, st_p = _mixer(xp, mod_p, l, wts, cos_p, sin_p, None, st_p,
                               R=1, Lr=MIX_TOKENS_PROMPT, Lg=SG_CHUNK)
        xp = _ffn(xp, mod_p, l, wts, R=1, Lr=FFN_TOKENS, n_sub=FFN_SUB)
        xs, m_s, st_s = _mixer(xs, mod_s, l, wts, cos_s, sin_s, state, st_s,
                               R=MIX_ROWS_SAMPLE, Lr=Ss, Lg=Ss)
        xs = _ffn(xs, mod_s, l, wts, R=FFN_TOKENS // Ss, Lr=Ss, n_sub=FFN_SUB)
        ms_p.append(m_p)
        ms_s.append(m_s)

    def unpack(st, ms, B, n_new):
        C, n, k, v = st[:4]
        m = jnp.stack(ms)[..., 0]
        k = _pair_layout(k).reshape(DEPTH, B, n_new, SW_KV, SW_DH)
        return C, n, m, k, v.reshape(DEPTH, B, n_new, SW_KV, SW_DH)

    C_p, n_p, m_p, k_p, v_p = unpack(st_p, ms_p, Bp, WINDOW)
    C_s, n_s, m_s, k_s, v_s = unpack(st_s, ms_s, Bs, Ss)
    return (xp, xs, C_p, n_p, m_p, k_p, v_p, C_s, n_s, m_s, k_s, v_s, st_s[4])
```

```python
import functools

import numpy as np
import jax
import jax.numpy as jnp
from jax import lax
from jax.experimental import pallas as pl
from jax.experimental.pallas import tpu as pltpu

F32 = jnp.float32
BF16 = jnp.bfloat16

D_MODEL = 1024
DEPTH = 4
PAST_LEN = 1024
ML_HEADS = 4
ML_DH = 128
ML_W = ML_HEADS * ML_DH
SG_GROUPS = 4
SG_CHUNK = 128
SG_W = 512
SG_GDIM = SG_W // SG_GROUPS
SW_HEADS = 8
SW_KV = 2
SW_REP = SW_HEADS // SW_KV
SW_DH = 64
SW_HALF = SW_DH // 2
SW_W = SW_HEADS * SW_DH
SW_KVW = SW_KV * SW_DH
SW_CHUNK = 64
WINDOW = 128
BAND = WINDOW + SW_CHUNK
ROPE_THETA = 10000.0
FFN_HIDDEN = 2816
EPS = 1e-6
SPLIT_SIZES = (ML_W, ML_W, ML_W, ML_W, ML_HEADS, ML_HEADS, SG_W, SG_W, SW_W, SW_KVW, SW_KVW, 3 * D_MODEL)

LANES = 128
N_GATE = 2 * ML_HEADS
GATE_ROWS = 16

OFF_MQ = 0
OFF_MV = OFF_MQ + ML_W
OFF_MO = OFF_MV + ML_W
OFF_SU = OFF_MO + ML_W
OFF_SV = OFF_SU + SG_W
OFF_WQ = OFF_SV + SG_W
OFF_WK = OFF_WQ + SW_W
OFF_WV = OFF_WK + SW_KVW
OFF_GL = OFF_WV + SW_KVW
D_IN_PAD = OFF_GL + 3 * D_MODEL

FFN_SPLITS = ((0, 1536), (1536, FFN_HIDDEN))

VMEM_LIMIT = 56 * 1024 * 1024

NT_DIMS = (((1,), (1,)), ((), ()))


def _dot(a, b):
    return jnp.dot(a, b, preferred_element_type=F32)


def _dot_nt(a, b):
    return lax.dot_general(a, b, NT_DIMS, preferred_element_type=F32)


def _sigmoid(x):
    return 0.5 * jnp.tanh(0.5 * x) + 0.5


def _rms(x):
    return x * lax.rsqrt(jnp.mean(x * x, axis=-1, keepdims=True) + EPS)


def _rows(m3, R, Lr):
    W = m3.shape[-1]
    if R == 1:
        return m3.reshape(1, W)
    return jnp.broadcast_to(m3, (R, Lr, W)).reshape(R * Lr, W)


def _split3(x):
    h1 = x.astype(BF16)
    r1 = x - h1.astype(F32)
    h2 = r1.astype(BF16)
    h3 = (r1 - h2.astype(F32)).astype(BF16)
    return h1, h2, h3


class _Fillers:
    def __init__(self, total_weight):
        self._thunks, self._vals, self._left = {}, {}, total_weight

    def add(self, name, thunk):
        self._thunks[name] = thunk

    def get(self, name):
        if name not in self._vals:
            self._vals[name] = self._thunks.pop(name)()
        return self._vals[name]

    def pump(self, weight):
        n = -(-len(self._thunks) * weight // max(self._left, 1))
        self._left -= weight
        for name in list(self._thunks)[:n]:
            self.get(name)

    def drain(self):
        for name in list(self._thunks):
            if name in self._thunks:
                self.get(name)


def _mod_kernel(c_ref, w_ref, b_ref, o_ref):
    c = c_ref[...]
    a = (c * _sigmoid(c)).astype(BF16)
    o_ref[...] = _dot(a, w_ref[...].astype(BF16)) + b_ref[...]


def _modulation(c_all, w_ada, b_ada):
    nb = c_all.shape[0]
    return pl.pallas_call(
        _mod_kernel,
        grid=(DEPTH, 6),
        in_specs=[
            pl.BlockSpec((nb, D_MODEL), lambda l, j: (0, 0)),
            pl.BlockSpec((None, D_MODEL, D_MODEL), lambda l, j: (l, 0, j)),
            pl.BlockSpec((None, None, 1, D_MODEL), lambda l, j: (l, j, 0, 0)),
        ],
        out_specs=pl.BlockSpec((None, None, nb, D_MODEL), lambda l, j: (l, j, 0, 0)),
        out_shape=jax.ShapeDtypeStruct((DEPTH, 6, nb, D_MODEL), F32),
        compiler_params=pltpu.CompilerParams(dimension_semantics=("parallel", "parallel")),
        name="adaln_modulation",
    )(c_all, w_ada, b_ada.reshape(DEPTH, 6, 1, D_MODEL))


def _rope(x, cos, sin_signed):
    T, W = x.shape
    outs = []
    for j in range(W // LANES):
        xb = x[:, j * LANES:(j + 1) * LANES]
        outs.append(xb * cos + pltpu.roll(xb, LANES // 2, 1) * sin_signed)
    return outs[0] if len(outs) == 1 else jnp.concatenate(outs, axis=1)


def _mlstm_unit(qb, kt, v, a_row, b_col, caug, m_prev):
    L = qb.shape[0]
    ti = lax.broadcasted_iota(jnp.int32, (L, L), 0)
    si = lax.broadcasted_iota(jnp.int32, (L, L), 1)
    a_tri = jnp.where(ti >= si, a_row, -jnp.inf)
    g = jnp.maximum(jnp.max(a_tri, axis=-1, keepdims=True), m_prev)
    w_intra = jnp.exp(a_tri - g)
    w_inter = jnp.exp(m_prev - g)
    s = _dot(qb, kt.astype(BF16)) * w_intra
    vaug = jnp.concatenate([v, jnp.ones_like(v)], axis=1).astype(BF16)
    qc = _dot(qb, caug.astype(BF16))
    sv = _dot(s.astype(BF16), vaug)
    num = w_inter * qc[:, :ML_DH] + sv[:, :ML_DH]
    den = w_inter * qc[:, ML_DH:] + sv[:, ML_DH:]
    floor = jnp.exp(-(jnp.broadcast_to(b_col, (L, ML_DH)) + g))
    h = num / jnp.maximum(jnp.abs(den), floor)
    g_last = g[L - 1:L, :]
    ws_row = jnp.exp(a_row - g_last)
    upd = _dot((kt * ws_row).astype(BF16), vaug)
    caug_new = jnp.exp(m_prev - g_last) * caug + upd
    return h, caug_new, b_col[L - 1:L, :] + g_last


def _swa_chunk(qstack, kg, vaug, sink_cols, mask):
    o = None
    e_cols = []
    for g in range(SW_KV):
        s = _dot_nt(qstack, kg[g])
        if mask is not None:
            s = jnp.where(mask, s, -jnp.inf)
        mx = jnp.maximum(jnp.max(s, axis=-1, keepdims=True), sink_cols[g])
        p = jnp.exp(s - mx).astype(BF16)
        e_cols.append(jnp.exp(sink_cols[g] - mx))
        og = _dot(p, vaug[g])
        o = og if o is None else o + og
    lane = lax.broadcasted_iota(jnp.int32, (qstack.shape[0], LANES), 1)
    e = jnp.where(lane < SW_DH, e_cols[0], e_cols[1])
    return o[:, :LANES] / (o[:, LANES:] + e)


def _mixer_kernel(*refs, R, Lr, Lg, has_state, n_seq, n_alias, n_sub):
    it = iter(refs)
    x_ref, sh1_ref, sc1_ref, g1_ref = next(it), next(it), next(it), next(it)
    win_ref, wkt_ref, wgt_ref, gpre_ref, gpost_ref = next(it), next(it), next(it), next(it), next(it)
    bgr_ref = next(it)
    lng_ref, lnb_ref, wsp_ref, bsp_ref = next(it), next(it), next(it), next(it)
    sinks_ref = next(it)
    wbm_ref, wbg_ref, wbs_ref, wout_ref = next(it), next(it), next(it), next(it)
    cos_ref, sin_ref = next(it), next(it)
    if has_state:
        cin_ref, nin_ref, min_ref, ck_ref, cv_ref = next(it), next(it), next(it), next(it), next(it)
    for _ in range(n_alias):
        next(it)
    xo_ref, cout_ref, nout_ref, m_ref, kn_ref, vn_ref = (next(it), next(it), next(it), next(it), next(it),
                                                         next(it))
    if has_state:
        gv_ref = next(it)
    st_ref, yml_ref, ysg_ref, ysw_ref = next(it), next(it), next(it), next(it)
    if not has_state:
        kprev_ref, vprev_ref = next(it), next(it)

    Ls = Lr // n_sub
    T = R * Ls
    seq = pl.program_id(1)

    @pl.when(seq == 0)
    def _init():
        if has_state:
            for r in range(R):
                for hd in range(ML_HEADS):
                    st_ref[r, hd, :, :ML_DH] = cin_ref[r, hd]
                    st_ref[r, hd, :, ML_DH:] = jnp.broadcast_to(nin_ref[r, hd:hd + 1, :], (ML_DH, ML_DH)).T
            m_ref[...] = min_ref[...]
        else:
            st_ref[...] = jnp.zeros_like(st_ref)
            m_ref[...] = jnp.zeros_like(m_ref)
            kprev_ref[...] = jnp.zeros_like(kprev_ref)
            vprev_ref[...] = jnp.zeros_like(vprev_ref)

    n_sw = Ls // SW_CHUNK
    w_ml, w_sg, w_sw = (3, 1, 3) if Ls > SW_CHUNK else (1, 1, 1)
    fill = _Fillers(n_sub * (R * ML_HEADS * w_ml + SG_GROUPS * w_sg + R * n_sw * w_sw))
    half = D_MODEL // 2

    def toks(k):
        return slice(k * Ls, (k + 1) * Ls)

    def load_x(k):
        return x_ref[:, toks(k), :].reshape(T, D_MODEL)

    def norm(k):
        h = _rms(load_x(k)) * gpre_ref[...] * (1.0 + _rows(sc1_ref[...], R, Ls)) + _rows(sh1_ref[...], R, Ls)
        return h.astype(BF16)

    def queue_head(k):
        hb = lambda: fill.get(("hb", k))
        fill.add(("grow", k), lambda: _dot_nt(wgt_ref[...], hb()) + bgr_ref[...])
        fill.add(("mq", k), lambda: _dot(hb(), win_ref[:, OFF_MQ:OFF_MQ + ML_W]).astype(BF16))
        fill.add(("mkt", k), lambda: _dot_nt(wkt_ref[...], hb()) * (ML_DH ** -0.5))
        fill.add(("mv", k), lambda: _dot(hb(), win_ref[:, OFF_MV:OFF_MV + ML_W]))

    def queue_body(k, with_gates):
        hb = lambda: fill.get(("hb", k))

        def proj(lo, width):
            return _dot(hb(), win_ref[:, lo:lo + width])

        def gmlp_inputs():
            sv = proj(OFF_SV, SG_W)
            mu = jnp.mean(sv, axis=-1, keepdims=True)
            svc = sv - mu
            var = jnp.mean(svc * svc, axis=-1, keepdims=True)
            return svc * lax.rsqrt(var + EPS) * lng_ref[...] + lnb_ref[...]

        def rope_tables():
            cos, sin = cos_ref[toks(k), :], sin_ref[toks(k), :]
            if R > 1:
                cos = jnp.concatenate([cos] * R, axis=0)
                sin = jnp.concatenate([sin] * R, axis=0)
            return cos, sin

        if not with_gates:
            fill.add(("mo", k), lambda: _sigmoid(proj(OFF_MO, ML_W)))
            fill.add(("su", k), lambda: proj(OFF_SU, SG_W))
            fill.add(("vnorm", k), gmlp_inputs)
            fill.add(("wq", k), lambda: (_rope(proj(OFF_WQ, SW_W), *rope_tables()) * (SW_DH ** -0.5)).astype(BF16))
            fill.add(("wk", k), lambda: _rope(proj(OFF_WK, SW_KVW), *rope_tables()))
            fill.add(("wv", k), lambda: proj(OFF_WV, SW_KVW))
        else:
            for i in range(6):
                fill.add(("gl", k, i), functools.partial(lambda i: _sigmoid(proj(OFF_GL + i * half, half)), i))

    def queue_tail(k):
        rows = slice(k * T, (k + 1) * T)
        gate = lambda b: jnp.concatenate([fill.get(("gl", k, 2 * b)), fill.get(("gl", k, 2 * b + 1))], axis=1)
        fill.add(("sw", k), lambda: _dot(ysw_ref[rows, :], wbs_ref[...]))

        def out_proj():
            merged = gate(0) * fill.get(("ml", k))
            merged = merged + gate(1) * fill.get(("sg", k))
            merged = merged + gate(2) * fill.get(("sw", k))
            return _dot(merged.astype(BF16), wout_ref[...])

        def emit():
            y = load_x(k) + _rows(g1_ref[...], R, Ls) * (_rms(fill.get(("o", k))) * gpost_ref[...])
            xo_ref[:, toks(k), :] = y.reshape(R, Ls, D_MODEL)
            return None

        fill.add(("o", k), out_proj)
        fill.add(("emit", k), emit)

    ri = lax.broadcasted_iota(jnp.int32, (Ls, Ls), 0)
    ci = lax.broadcasted_iota(jnp.int32, (Ls, Ls), 1)
    triu = jnp.where(ri <= ci, 1.0, 0.0).astype(BF16)
    triu3 = jnp.concatenate([triu] * 3, axis=0)
    gi = lax.broadcasted_iota(jnp.int32, (Lg, Lg), 0)
    gj = lax.broadcasted_iota(jnp.int32, (Lg, Lg), 1)
    n_q = SW_REP * SW_CHUNK
    rid = lax.broadcasted_iota(jnp.int32, (n_q, 1), 0) // SW_CHUNK
    sink_cols = []
    for g in range(SW_KV):
        col = jnp.full((n_q, 1), sinks_ref[SW_REP * g + SW_REP - 1], F32)
        for i in range(SW_REP - 2, -1, -1):
            col = jnp.where(rid == i, sinks_ref[SW_REP * g + i], col)
        sink_cols.append(col)
    kpos_chunk = lax.broadcasted_iota(jnp.int32, (n_q, BAND), 1) // SW_CHUNK

    def mixers(k):
        rows = slice(k * T, (k + 1) * T)
        grow = fill.get(("grow", k))
        lf_row = jax.nn.log_sigmoid(grow)
        mqb, mkt, mv = fill.get(("mq", k)), fill.get(("mkt", k)), fill.get(("mv", k))
        for r in range(R):
            rs = slice(r * Ls, (r + 1) * Ls)
            brow_all = _dot(jnp.concatenate(_split3(lf_row[:, rs]), axis=1), triu3)
            bcol_all = brow_all.T
            for hd in range(ML_HEADS):
                hs = slice(hd * ML_DH, (hd + 1) * ML_DH)
                fg = ML_HEADS + hd
                h_ml, caug_new, m_new = _mlstm_unit(
                    mqb[rs, hs], mkt[hs, rs], mv[rs, hs],
                    grow[hd:hd + 1, rs] - brow_all[fg:fg + 1, :], bcol_all[:, fg:fg + 1],
                    st_ref[r, hd], m_ref[r, hd:hd + 1, 0:1])
                st_ref[r, hd] = caug_new
                m_ref[r, hd:hd + 1, :] = jnp.broadcast_to(m_new, (1, LANES))
                fill.pump(w_ml)
                yml_ref[k * T + r * Ls:k * T + (r + 1) * Ls, hs] = (
                    fill.get(("mo", k))[rs, hs] * h_ml).astype(BF16)
        fill.add(("ml", k), lambda: _dot(yml_ref[rows, :], wbm_ref[...]))

        vnorm = fill.get(("vnorm", k))
        su = fill.get(("su", k))
        if has_state:
            gv_ref[...] = vnorm.reshape(R, Ls, SG_W)
        n_chunk = T // Lg
        for g in range(SG_GROUPS):
            cs = slice(g * SG_GDIM, (g + 1) * SG_GDIM)
            w = jnp.where(gi >= gj, wsp_ref[g, :Lg, :Lg], 0.0).astype(BF16)
            vcat = jnp.concatenate([vnorm[c * Lg:(c + 1) * Lg, cs] for c in range(n_chunk)], axis=1)
            z = _dot(w, vcat.astype(BF16))
            for c in range(n_chunk):
                ts = slice(c * Lg, (c + 1) * Lg)
                zc = z[:, c * SG_GDIM:(c + 1) * SG_GDIM] + bsp_ref[g, :Lg, :]
                ysg_ref[k * T + c * Lg:k * T + (c + 1) * Lg, cs] = (su[ts, cs] * zc).astype(BF16)
            fill.pump(w_sg)
        fill.add(("sg", k), lambda: _dot(ysg_ref[rows, :], wbg_ref[...]))

        wqb, wk, wv = fill.get(("wq", k)), fill.get(("wk", k)), fill.get(("wv", k))
        for r in range(R):
            if has_state:
                kband = jnp.concatenate([ck_ref[r], wk[r * Ls:(r + 1) * Ls]], axis=0)
                vband = jnp.concatenate([cv_ref[r], wv[r * Ls:(r + 1) * Ls]], axis=0)
            else:
                kband = jnp.concatenate([kprev_ref[...], wk], axis=0)
                vband = jnp.concatenate([vprev_ref[...], wv], axis=0)
            lane = lax.broadcasted_iota(jnp.int32, kband.shape, 1)
            k_group = (lane // SW_HALF) % SW_KV
            v_group = lane // SW_DH
            kg = [jnp.where(k_group == g, kband, 0.0).astype(BF16) for g in range(SW_KV)]
            vaug = [jnp.concatenate([jnp.where(v_group == g, vband, 0.0),
                                     jnp.where(v_group == g, 1.0, 0.0)], axis=1).astype(BF16)
                    for g in range(SW_KV)]
            for j in range(n_sw):
                t0 = r * Ls + j * SW_CHUNK
                qstack = jnp.concatenate([wqb[t0:t0 + SW_CHUNK, i * LANES:(i + 1) * LANES]
                                          for i in range(SW_REP)], axis=0)
                needs_mask = not has_state and k == 0 and j < WINDOW // SW_CHUNK
                mask = kpos_chunk >= (WINDOW // SW_CHUNK) - (seq * n_sub * n_sw + j) if needs_mask else None
                ks = slice(j * SW_CHUNK, j * SW_CHUNK + BAND)
                y = _swa_chunk(qstack, [a[ks] for a in kg], [a[ks] for a in vaug], sink_cols, mask)
                for i in range(SW_REP):
                    ysw_ref[k * T + t0:k * T + t0 + SW_CHUNK, i * LANES:(i + 1) * LANES] = (
                        y[i * SW_CHUNK:(i + 1) * SW_CHUNK].astype(BF16))
                fill.pump(w_sw)
        if has_state:
            kn_ref[...] = wk.reshape(R, Ls, SW_KVW)
            vn_ref[...] = wv.reshape(R, Ls, SW_KVW)
        else:
            kprev_ref[...] = wk[T - WINDOW:]
            vprev_ref[...] = wv[T - WINDOW:]

    fill.add(("hb", 0), lambda: norm(0))
    queue_head(0)
    for k in range(n_sub):
        for name in ("grow", "mq", "mkt", "mv"):
            fill.get((name, k))
        queue_body(k, with_gates=False)
        if k + 1 < n_sub:
            fill.add(("hb", k + 1), functools.partial(norm, k + 1))
            fill.get(("hb", k + 1))
            queue_head(k + 1)
        queue_body(k, with_gates=True)
        mixers(k)
        queue_tail(k)
    fill.drain()

    @pl.when(seq == n_seq - 1)
    def _emit_state():
        if not has_state:
            kn_ref[0] = kprev_ref[...]
            vn_ref[0] = vprev_ref[...]
        for r in range(R):
            for hd in range(ML_HEADS):
                cout_ref[r, hd] = st_ref[r, hd, :, :ML_DH]
                nout_ref[r, hd:hd + 1, :] = st_ref[r, hd, :, ML_DH:].T[0:1, :]


def _const_spec(shape, layer):
    nd = len(shape)
    return pl.BlockSpec((None,) + tuple(shape), lambda b, s: (layer,) + (0,) * nd,
                        pipeline_mode=pl.Buffered(1))


def _mixer(x, mod, layer, wts, cos_t, sin_t, state, prev, *, R, Lr, Lg, n_sub):
    B, S, _ = x.shape
    n_seq = S // Lr
    has_state = state is not None
    T = R * Lr
    assert n_sub == 1 or R == 1

    def mod_spec(j):
        return pl.BlockSpec((None, None, R, 1, D_MODEL), lambda b, s: (layer, j, b, 0, 0))

    in_specs = [
        pl.BlockSpec((R, Lr, D_MODEL), lambda b, s: (b, s, 0)),
        mod_spec(0), mod_spec(1), mod_spec(2),
        _const_spec((D_MODEL, D_IN_PAD), layer),
        _const_spec((ML_W, D_MODEL), layer),
        _const_spec((GATE_ROWS, D_MODEL), layer),
        _const_spec((1, D_MODEL), layer),
        _const_spec((1, D_MODEL), layer),
        _const_spec((GATE_ROWS, 1), layer),
        _const_spec((1, SG_W), layer),
        _const_spec((1, SG_W), layer),
        _const_spec((SG_GROUPS, SG_CHUNK, SG_CHUNK), layer),
        _const_spec((SG_GROUPS, SG_CHUNK, SG_GDIM), layer),
        pl.BlockSpec(memory_space=pltpu.SMEM),
        _const_spec((ML_W, D_MODEL), layer),
        _const_spec((SG_W, D_MODEL), layer),
        _const_spec((SW_W, D_MODEL), layer),
        _const_spec((D_MODEL, D_MODEL), layer),
        pl.BlockSpec((Lr, LANES), (lambda b, s: (0, 0)) if has_state else (lambda b, s: (s, 0))),
        pl.BlockSpec((Lr, LANES), (lambda b, s: (0, 0)) if has_state else (lambda b, s: (s, 0))),
    ]
    args = [x, mod, mod, mod, wts["win"], wts["wkt"], wts["wgt"], wts["g_pre_mix"], wts["g_post_mix"],
            wts["bg_row"], wts["ln_v_g"], wts["ln_v_b"], wts["w_spatial"], wts["b_spatial_b"],
            wts["swa_sinks"][layer], wts["w_br_mlstm"], wts["w_br_gmlp"], wts["w_br_swa"], wts["w_out"],
            cos_t, sin_t]
    n_new = Lr if has_state else WINDOW
    stacked = [
        ((DEPTH, B, ML_HEADS, ML_DH, ML_DH), (None, R, ML_HEADS, ML_DH, ML_DH), lambda b, s: (layer, b, 0, 0, 0)),
        ((DEPTH, B, ML_HEADS, ML_DH), (None, R, ML_HEADS, ML_DH), lambda b, s: (layer, b, 0, 0)),
        ((DEPTH, B, n_new, SW_KVW), (None, R, n_new, SW_KVW), lambda b, s: (layer, b, 0, 0)),
        ((DEPTH, B, n_new, SW_KVW), (None, R, n_new, SW_KVW), lambda b, s: (layer, b, 0, 0)),
    ]
    if has_state:
        stacked.append(((DEPTH, B, S, SG_W), (None, R, Lr, SG_W), lambda b, s: (layer, b, s, 0)))
    st_shapes = [jax.ShapeDtypeStruct(shp, F32) for shp, _, _ in stacked]
    st_specs = [pl.BlockSpec(blk, imap) for _, blk, imap in stacked]
    x_out = (jax.ShapeDtypeStruct((B, S, D_MODEL), F32), pl.BlockSpec((R, Lr, D_MODEL), lambda b, s: (b, s, 0)))
    m_out = (jax.ShapeDtypeStruct((B, ML_HEADS, LANES), F32),
             pl.BlockSpec((R, ML_HEADS, LANES), lambda b, s: (b, 0, 0)))
    out_shape = [x_out[0], st_shapes[0], st_shapes[1], m_out[0]] + st_shapes[2:]
    out_specs = [x_out[1], st_specs[0], st_specs[1], m_out[1]] + st_specs[2:]
    stacked_out_idx = [1, 2] + list(range(4, 4 + len(stacked) - 2))
    scratch = [pltpu.VMEM((R, ML_HEADS, ML_DH, 2 * ML_DH), F32),
               pltpu.VMEM((T, ML_W), BF16), pltpu.VMEM((T, SG_W), BF16), pltpu.VMEM((T, SW_W), BF16)]
    if has_state:
        c_in, n_in, m_in, cache_k, cache_v = state
        in_specs += [
            pl.BlockSpec((None, R, ML_HEADS, ML_DH, ML_DH), lambda b, s: (layer, b, 0, 0, 0)),
            pl.BlockSpec((None, R, ML_HEADS, ML_DH), lambda b, s: (layer, b, 0, 0)),
            pl.BlockSpec((None, R, ML_HEADS, LANES), lambda b, s: (layer, b, 0, 0)),
            pl.BlockSpec((None, R, WINDOW, SW_KVW), lambda b, s: (layer, b, 0, 0)),
            pl.BlockSpec((None, R, WINDOW, SW_KVW), lambda b, s: (layer, b, 0, 0)),
        ]
        args += [c_in, n_in, m_in, cache_k, cache_v]
    else:
        scratch += [pltpu.VMEM((WINDOW, SW_KVW), F32), pltpu.VMEM((WINDOW, SW_KVW), F32)]
    aliases = {}
    if prev is not None:
        for arr, oi in zip(prev, stacked_out_idx):
            aliases[len(args)] = oi
            in_specs.append(pl.BlockSpec(memory_space=pl.ANY))
            args.append(arr)

    outs = pl.pallas_call(
        functools.partial(_mixer_kernel, R=R, Lr=Lr, Lg=Lg, has_state=has_state, n_seq=n_seq,
                          n_alias=len(aliases), n_sub=n_sub),
        grid=(B // R, n_seq),
        in_specs=in_specs,
        out_specs=out_specs,
        out_shape=out_shape,
        scratch_shapes=scratch,
        input_output_aliases=aliases,
        compiler_params=pltpu.CompilerParams(
            dimension_semantics=("parallel", "arbitrary"), vmem_limit_bytes=VMEM_LIMIT),
        name="mixer_sample" if has_state else "mixer_prompt",
    )(*args)
    return outs[0], outs[3], [outs[i] for i in stacked_out_idx]


def _ffn_kernel(x_ref, sh_ref, sc_ref, g_ref, gpre_ref, gpost_ref, wfi_ref, wfo_ref, o_ref, *, R, Lr, n_sub):
    Rs = R // n_sub if R > 1 else 1
    Ls = Lr if R > 1 else Lr // n_sub
    Ts = Rs * Ls

    def rows_of(ref, k):
        return _rows(ref[k * Rs:(k + 1) * Rs] if R > 1 else ref[...], Rs, Ls)

    def load(k):
        xk = x_ref[k * Rs:(k + 1) * Rs] if R > 1 else x_ref[:, k * Ls:(k + 1) * Ls, :]
        return xk.reshape(Ts, D_MODEL)

    def prologue(k):
        x = load(k)
        h = _rms(x) * gpre_ref[...] * (1.0 + rows_of(sc_ref, k)) + rows_of(sh_ref, k)
        return h.astype(BF16)

    def matmuls(hb):
        f = None
        for lo, hi in FFN_SPLITS:
            gate = _dot(hb, wfi_ref[:, lo:hi])
            up = _dot(hb, wfi_ref[:, FFN_HIDDEN + lo:FFN_HIDDEN + hi])
            act = (gate * _sigmoid(gate) * up).astype(BF16)
            part = _dot(act, wfo_ref[lo:hi, :])
            f = part if f is None else f + part
        return f

    def epilogue(k, f):
        y = (load(k) + rows_of(g_ref, k) * (_rms(f) * gpost_ref[...])).reshape(Rs, Ls, D_MODEL)
        if R > 1:
            o_ref[k * Rs:(k + 1) * Rs] = y
        else:
            o_ref[:, k * Ls:(k + 1) * Ls, :] = y

    hb = prologue(0)
    for k in range(n_sub):
        hb_next = prologue(k + 1) if k + 1 < n_sub else None
        f = matmuls(hb)
        epilogue(k, f)
        hb = hb_next


def _ffn(x, mod, layer, wts, *, R, Lr, n_sub):
    B, S, _ = x.shape

    def mod_spec(j):
        return pl.BlockSpec((None, None, R, 1, D_MODEL), lambda b, s: (layer, j, b, 0, 0))

    return pl.pallas_call(
        functools.partial(_ffn_kernel, R=R, Lr=Lr, n_sub=n_sub),
        grid=(B // R, S // Lr),
        in_specs=[
            pl.BlockSpec((R, Lr, D_MODEL), lambda b, s: (b, s, 0)),
            mod_spec(3), mod_spec(4), mod_spec(5),
            _const_spec((1, D_MODEL), layer),
            _const_spec((1, D_MODEL), layer),
            _const_spec((D_MODEL, 2 * FFN_HIDDEN), layer),
            _const_spec((FFN_HIDDEN, D_MODEL), layer),
        ],
        out_specs=pl.BlockSpec((R, Lr, D_MODEL), lambda b, s: (b, s, 0)),
        out_shape=jax.ShapeDtypeStruct((B, S, D_MODEL), F32),
        compiler_params=pltpu.CompilerParams(
            dimension_semantics=("parallel", "parallel"), vmem_limit_bytes=VMEM_LIMIT),
        name="ffn",
    )(x, mod, mod, mod, wts["g_pre_ffn"], wts["g_post_ffn"], wts["w_ffn_in"], wts["w_ffn_out"])


def _pair_layout(a):
    lead = a.shape[:-1]
    return jnp.swapaxes(a.reshape(lead + (2, 2, SW_HALF)), -3, -2).reshape(lead + (2 * SW_DH,))


def _query_layout(wq):
    lead = wq.shape[:-1]
    w = wq.reshape(lead + (SW_KV, SW_REP, 2, SW_HALF))
    return jnp.moveaxis(w, -4, -2).reshape(lead + (SW_W,))


def _rope_tables(pos):
    inv = ROPE_THETA ** (-jnp.arange(SW_HALF, dtype=F32) / SW_HALF)
    ang = pos.astype(F32)[:, None] * inv[None, :]
    cos, sin = jnp.cos(ang), jnp.sin(ang)
    cos_t = jnp.concatenate([cos, cos, cos, cos], axis=1)
    sin_t = jnp.concatenate([-sin, -sin, sin, sin], axis=1)
    return cos_t, sin_t


def _prep_weights(w_in, b_igate, b_fgate, g_pre_mix, g_post_mix, g_pre_ffn, g_post_ffn, ln_v_g, ln_v_b,
                  w_spatial, b_spatial, swa_sinks, w_br_mlstm, w_br_gmlp, w_br_swa, w_out, w_ffn_in, w_ffn_out):
    offs = np.cumsum((0,) + SPLIT_SIZES)
    seg = [w_in[..., offs[i]:offs[i + 1]] for i in range(len(SPLIT_SIZES))]
    mq, mk, mv, mo, mi, mf, su, sv, wq, wk, wv, gl = seg
    gates = jnp.concatenate([mi, mf], axis=-1)
    win = jnp.concatenate([mq, mv, mo, su, sv, _query_layout(wq), _pair_layout(wk), wv, gl],
                          axis=-1).astype(BF16)
    bg = jnp.concatenate([b_igate, b_fgate], axis=-1)
    row = lambda a: a.reshape(DEPTH, 1, a.shape[-1])
    wbs = jnp.swapaxes(w_br_swa.reshape(DEPTH, SW_KV, SW_REP, SW_DH, D_MODEL), 1, 2).reshape(DEPTH, SW_W, D_MODEL)
    return dict(
        win=win,
        wkt=jnp.swapaxes(mk, 1, 2).astype(BF16),
        wgt=jnp.pad(jnp.swapaxes(gates, 1, 2), ((0, 0), (0, GATE_ROWS - N_GATE), (0, 0))).astype(BF16),
        bg_row=jnp.pad(bg, ((0, 0), (0, GATE_ROWS - N_GATE))).reshape(DEPTH, GATE_ROWS, 1),
        g_pre_mix=row(g_pre_mix), g_post_mix=row(g_post_mix),
        g_pre_ffn=row(g_pre_ffn), g_post_ffn=row(g_post_ffn),
        ln_v_g=row(ln_v_g), ln_v_b=row(ln_v_b),
        w_spatial=w_spatial,
        b_spatial_b=jnp.broadcast_to(b_spatial[..., None], b_spatial.shape + (SG_GDIM,)),
        swa_sinks=swa_sinks,
        w_br_mlstm=w_br_mlstm.astype(BF16), w_br_gmlp=w_br_gmlp.astype(BF16),
        w_br_swa=wbs.astype(BF16), w_out=w_out.astype(BF16),
        w_ffn_in=w_ffn_in.astype(BF16), w_ffn_out=w_ffn_out.astype(BF16),
    )


MIX_TOKENS_PROMPT = 512
MIX_SUB_PROMPT = 2
MIX_ROWS_SAMPLE = 4
FFN_TOKENS = 1024
FFN_SUB = 2


def kernel(x_prompt, x_sample, c_prompt, c_sample, state_mlstm_C, state_mlstm_n, state_mlstm_m, cache_swa_k, cache_swa_v, w_ada, b_ada, g_pre_mix, g_post_mix, g_pre_ffn, g_post_ffn, w_in, b_igate, b_fgate, ln_v_g, ln_v_b, w_spatial, b_spatial, swa_sinks, w_br_mlstm, w_br_gmlp, w_br_swa, w_out, w_ffn_in, w_ffn_out):
    Bp, Sp, _ = x_prompt.shape
    Bs, Ss, _ = x_sample.shape
    wts = _prep_weights(w_in, b_igate, b_fgate, g_pre_mix, g_post_mix, g_pre_ffn, g_post_ffn, ln_v_g, ln_v_b,
                        w_spatial, b_spatial, swa_sinks, w_br_mlstm, w_br_gmlp, w_br_swa, w_out,
                        w_ffn_in, w_ffn_out)
    mod = _modulation(jnp.concatenate([c_prompt, c_sample], axis=0), w_ada, b_ada)
    mod = mod.reshape(DEPTH, 6, Bp + Bs, 1, D_MODEL)
    mod_p, mod_s = mod[:, :, :Bp], mod[:, :, Bp:]
    cos_p, sin_p = _rope_tables(jnp.arange(Sp))
    cos_s, sin_s = _rope_tables(PAST_LEN + jnp.arange(Ss))

    m_in = jnp.broadcast_to(state_mlstm_m[..., None], state_mlstm_m.shape + (LANES,))
    ck = _pair_layout(cache_swa_k.reshape(DEPTH, Bs, WINDOW, SW_KVW))
    cv = cache_swa_v.reshape(DEPTH, Bs, WINDOW, SW_KVW)
    state = (state_mlstm_C, state_mlstm_n, m_in, ck, cv)

    xp, xs = x_prompt, x_sample
    st_p, st_s, ms_p, ms_s = None, None, [], []
    for l in range(DEPTH):
        xp, m_p, st_p = _mixer(xp, mod_p, l, wts, cos_p, sin_p, None, st_p,
                               R=1, Lr=MIX_TOKENS_PROMPT, Lg=SG_CHUNK, n_sub=MIX_SUB_PROMPT)
        xp = _ffn(xp, mod_p, l, wts, R=1, Lr=FFN_TOKENS, n_sub=FFN_SUB)
        xs, m_s, st_s = _mixer(xs, mod_s, l, wts, cos_s, sin_s, state, st_s,
                               R=MIX_ROWS_SAMPLE, Lr=Ss, Lg=Ss, n_sub=1)
        xs = _ffn(xs, mod_s, l, wts, R=FFN_TOKENS // Ss, Lr=Ss, n_sub=FFN_SUB)
        ms_p.append(m_p)
        ms_s.append(m_s)

    def unpack(st, ms, B, n_new):
        C, n, k, v = st[:4]
        m = jnp.stack(ms)[..., 0]
        k = _pair_layout(k).reshape(DEPTH, B, n_new, SW_KV, SW_DH)
        return C, n, m, k, v.reshape(DEPTH, B, n_new, SW_KV, SW_DH)

    C_p, n_p, m_p, k_p, v_p = unpack(st_p, ms_p, Bp, WINDOW)
    C_s, n_s, m_s, k_s, v_s = unpack(st_s, ms_s, Bs, Ss)
    return (xp, xs, C_p, n_p, m_p, k_p, v_p, C_s, n_s, m_s, k_s, v_s, st_s[4])
```

```python
import functools

import numpy as np
import jax
import jax.numpy as jnp
from jax import lax
from jax.experimental import pallas as pl
from jax.experimental.pallas import tpu as pltpu

F32 = jnp.float32
BF16 = jnp.bfloat16

D_MODEL = 1024
DEPTH = 4
PAST_LEN = 1024
ML_HEADS = 4
ML_DH = 128
ML_W = ML_HEADS * ML_DH
SG_GROUPS = 4
SG_CHUNK = 128
SG_W = 512
SG_GDIM = SG_W // SG_GROUPS
SW_HEADS = 8
SW_KV = 2
SW_REP = SW_HEADS // SW_KV
SW_DH = 64
SW_HALF = SW_DH // 2
SW_W = SW_HEADS * SW_DH
SW_KVW = SW_KV * SW_DH
SW_CHUNK = 64
WINDOW = 128
BAND = WINDOW + SW_CHUNK
ROPE_THETA = 10000.0
FFN_HIDDEN = 2816
EPS = 1e-6
SPLIT_SIZES = (ML_W, ML_W, ML_W, ML_W, ML_HEADS, ML_HEADS, SG_W, SG_W, SW_W, SW_KVW, SW_KVW, 3 * D_MODEL)

LANES = 128
N_GATE = 2 * ML_HEADS
GATE_ROWS = 16

OFF_MQ = 0
OFF_MV = OFF_MQ + ML_W
OFF_MO = OFF_MV + ML_W
OFF_SU = OFF_MO + ML_W
OFF_SV = OFF_SU + SG_W
OFF_WQ = OFF_SV + SG_W
OFF_WK = OFF_WQ + SW_W
OFF_WV = OFF_WK + SW_KVW
OFF_GL = OFF_WV + SW_KVW
D_IN_PAD = OFF_GL + 3 * D_MODEL

FFN_SPLITS = ((0, 1536), (1536, FFN_HIDDEN))

VMEM_LIMIT = 56 * 1024 * 1024

NT_DIMS = (((1,), (1,)), ((), ()))


def _dot(a, b):
    return jnp.dot(a, b, preferred_element_type=F32)


def _dot_nt(a, b):
    return lax.dot_general(a, b, NT_DIMS, preferred_element_type=F32)


def _sigmoid(x):
    return 0.5 * jnp.tanh(0.5 * x) + 0.5


def _sigmoid_of_half(z):
    return 0.5 * jnp.tanh(z) + 0.5


def _rms(x):
    return x * lax.rsqrt(jnp.mean(x * x, axis=-1, keepdims=True) + EPS)


def _rows(m3, R, Lr):
    W = m3.shape[-1]
    if R == 1:
        return m3.reshape(1, W)
    return jnp.broadcast_to(m3, (R, Lr, W)).reshape(R * Lr, W)


def _split3(x):
    h1 = x.astype(BF16)
    r1 = x - h1.astype(F32)
    h2 = r1.astype(BF16)
    h3 = (r1 - h2.astype(F32)).astype(BF16)
    return h1, h2, h3


class _Fillers:
    def __init__(self, total_weight):
        self._thunks, self._vals, self._left = {}, {}, total_weight

    def add(self, name, thunk):
        self._thunks[name] = thunk

    def get(self, name):
        if name not in self._vals:
            self._vals[name] = self._thunks.pop(name)()
        return self._vals[name]

    def pump(self, weight):
        n = -(-len(self._thunks) * weight // max(self._left, 1))
        self._left -= weight
        for name in list(self._thunks)[:n]:
            self.get(name)

    def drain(self):
        for name in list(self._thunks):
            if name in self._thunks:
                self.get(name)


def _mod_kernel(c_ref, w_ref, b_ref, o_ref):
    c = c_ref[...]
    a = (c * _sigmoid(c)).astype(BF16)
    o_ref[...] = _dot(a, w_ref[...].astype(BF16)) + b_ref[...]


def _modulation(c_all, w_ada, b_ada):
    nb = c_all.shape[0]
    return pl.pallas_call(
        _mod_kernel,
        grid=(DEPTH, 6),
        in_specs=[
            pl.BlockSpec((nb, D_MODEL), lambda l, j: (0, 0)),
            pl.BlockSpec((None, D_MODEL, D_MODEL), lambda l, j: (l, 0, j)),
            pl.BlockSpec((None, None, 1, D_MODEL), lambda l, j: (l, j, 0, 0)),
        ],
        out_specs=pl.BlockSpec((None, None, nb, D_MODEL), lambda l, j: (l, j, 0, 0)),
        out_shape=jax.ShapeDtypeStruct((DEPTH, 6, nb, D_MODEL), F32),
        compiler_params=pltpu.CompilerParams(dimension_semantics=("parallel", "parallel")),
        name="adaln_modulation",
    )(c_all, w_ada, b_ada.reshape(DEPTH, 6, 1, D_MODEL))


def _rope(x, cos, sin_signed):
    T, W = x.shape
    outs = []
    for j in range(W // LANES):
        xb = x[:, j * LANES:(j + 1) * LANES]
        outs.append(xb * cos + pltpu.roll(xb, LANES // 2, 1) * sin_signed)
    return outs[0] if len(outs) == 1 else jnp.concatenate(outs, axis=1)


def _mlstm_unit(qb, kt, v, a_row, b_col, caug, m_prev):
    L = qb.shape[0]
    ti = lax.broadcasted_iota(jnp.int32, (L, L), 0)
    si = lax.broadcasted_iota(jnp.int32, (L, L), 1)
    a_tri = jnp.where(ti >= si, a_row, -jnp.inf)
    g = jnp.maximum(jnp.max(a_tri, axis=-1, keepdims=True), m_prev)
    w_intra = jnp.exp(a_tri - g)
    w_inter = jnp.exp(m_prev - g)
    s = _dot(qb, kt.astype(BF16)) * w_intra
    vaug = jnp.concatenate([v, jnp.ones_like(v)], axis=1).astype(BF16)
    qc = _dot(qb, caug.astype(BF16))
    sv = _dot(s.astype(BF16), vaug)
    num = w_inter * qc[:, :ML_DH] + sv[:, :ML_DH]
    den = w_inter * qc[:, ML_DH:] + sv[:, ML_DH:]
    floor = jnp.exp(-(jnp.broadcast_to(b_col, (L, ML_DH)) + g))
    h = num / jnp.maximum(jnp.abs(den), floor)
    g_last = g[L - 1:L, :]
    ws_row = jnp.exp(a_row - g_last)
    upd = _dot((kt * ws_row).astype(BF16), vaug)
    caug_new = jnp.exp(m_prev - g_last) * caug + upd
    return h, caug_new, b_col[L - 1:L, :] + g_last


def _swa_chunk(qstack, kg, vaug, sink_cols, mask):
    o = None
    e_cols = []
    for g in range(SW_KV):
        s = _dot_nt(qstack, kg[g])
        if mask is not None:
            s = jnp.where(mask, s, -jnp.inf)
        mx = jnp.maximum(jnp.max(s, axis=-1, keepdims=True), sink_cols[g])
        p = jnp.exp(s - mx).astype(BF16)
        e_cols.append(jnp.exp(sink_cols[g] - mx))
        og = _dot(p, vaug[g])
        o = og if o is None else o + og
    lane = lax.broadcasted_iota(jnp.int32, (qstack.shape[0], LANES), 1)
    e = jnp.where(lane < SW_DH, e_cols[0], e_cols[1])
    return o[:, :LANES] / (o[:, LANES:] + e)


def _mixer_kernel(*refs, R, Lr, Lg, has_state, n_seq, n_alias, n_sub):
    it = iter(refs)
    x_ref, sh1_ref, sc1_ref, g1_ref = next(it), next(it), next(it), next(it)
    win_ref, wkt_ref, gpre_ref, gpost_ref = next(it), next(it), next(it), next(it)
    bgr_ref = next(it)
    lng_ref, lnb_ref, wsp_ref, bsp_ref = next(it), next(it), next(it), next(it)
    sinks_ref = next(it)
    wbm_ref, wbg_ref, wbs_ref, wout_ref = next(it), next(it), next(it), next(it)
    cos_ref, sin_ref = next(it), next(it)
    if has_state:
        cin_ref, nin_ref, min_ref, ck_ref, cv_ref = next(it), next(it), next(it), next(it), next(it)
    for _ in range(n_alias):
        next(it)
    xo_ref, cout_ref, nout_ref, m_ref, kn_ref, vn_ref = (next(it), next(it), next(it), next(it), next(it),
                                                         next(it))
    if has_state:
        gv_ref = next(it)
    st_ref, yml_ref, ysg_ref, ysw_ref = next(it), next(it), next(it), next(it)
    if not has_state:
        kprev_ref, vprev_ref = next(it), next(it)

    Ls = Lr // n_sub
    T = R * Ls
    seq = pl.program_id(1)

    @pl.when(seq == 0)
    def _init():
        if has_state:
            for r in range(R):
                for hd in range(ML_HEADS):
                    st_ref[r, hd, :, :ML_DH] = cin_ref[r, hd]
                    st_ref[r, hd, :, ML_DH:] = jnp.broadcast_to(nin_ref[r, hd:hd + 1, :], (ML_DH, ML_DH)).T
            m_ref[...] = min_ref[...]
        else:
            st_ref[...] = jnp.zeros_like(st_ref)
            m_ref[...] = jnp.zeros_like(m_ref)
            kprev_ref[...] = jnp.zeros_like(kprev_ref)
            vprev_ref[...] = jnp.zeros_like(vprev_ref)

    n_sw = Ls // SW_CHUNK
    w_ml, w_sg, w_sw = (3, 1, 3) if Ls > SW_CHUNK else (1, 1, 1)
    fill = _Fillers(n_sub * (R * ML_HEADS * w_ml + SG_GROUPS * w_sg + R * n_sw * w_sw))
    half = D_MODEL // 2

    def toks(k):
        return slice(k * Ls, (k + 1) * Ls)

    def load_x(k):
        return x_ref[:, toks(k), :].reshape(T, D_MODEL)

    def norm(k):
        scale = gpre_ref[...] * (1.0 + _rows(sc1_ref[...], R, Ls))
        return (_rms(load_x(k)) * scale + _rows(sh1_ref[...], R, Ls)).astype(BF16)

    def queue_head(k):
        hb = lambda: fill.get(("hb", k))
        fill.add(("kg", k), lambda: _dot_nt(wkt_ref[...], hb()))
        fill.add(("grow", k), lambda: fill.get(("kg", k))[ML_W:, :] + bgr_ref[...])
        fill.add(("mq", k), lambda: _dot(hb(), win_ref[:, OFF_MQ:OFF_MQ + ML_W]).astype(BF16))
        fill.add(("mkt", k), lambda: fill.get(("kg", k))[:ML_W, :] * (ML_DH ** -0.5))
        fill.add(("mv", k), lambda: _dot(hb(), win_ref[:, OFF_MV:OFF_MV + ML_W]))

    def queue_body(k, with_gates):
        hb = lambda: fill.get(("hb", k))

        def proj(lo, width):
            return _dot(hb(), win_ref[:, lo:lo + width])

        def gmlp_inputs():
            sv = proj(OFF_SV, SG_W)
            mu = jnp.mean(sv, axis=-1, keepdims=True)
            svc = sv - mu
            var = jnp.mean(svc * svc, axis=-1, keepdims=True)
            return svc * lax.rsqrt(var + EPS) * lng_ref[...] + lnb_ref[...]

        def rope_tables():
            cos, sin = cos_ref[toks(k), :], sin_ref[toks(k), :]
            if R > 1:
                cos = jnp.concatenate([cos] * R, axis=0)
                sin = jnp.concatenate([sin] * R, axis=0)
            return cos, sin

        if not with_gates:
            fill.add(("mo", k), lambda: _sigmoid_of_half(proj(OFF_MO, ML_W)))
            fill.add(("su", k), lambda: proj(OFF_SU, SG_W))
            fill.add(("vnorm", k), gmlp_inputs)
            fill.add(("wq", k), lambda: (_rope(proj(OFF_WQ, SW_W), *rope_tables()) * (SW_DH ** -0.5)).astype(BF16))
            fill.add(("wk", k), lambda: _rope(proj(OFF_WK, SW_KVW), *rope_tables()))
            fill.add(("wv", k), lambda: proj(OFF_WV, SW_KVW))
        else:
            for i in range(6):
                fill.add(("gl", k, i), functools.partial(
                    lambda i: _sigmoid_of_half(proj(OFF_GL + i * half, half)), i))

    def queue_tail(k):
        rows = slice(k * T, (k + 1) * T)
        gate = lambda b: jnp.concatenate([fill.get(("gl", k, 2 * b)), fill.get(("gl", k, 2 * b + 1))], axis=1)
        fill.add(("sw", k), lambda: _dot(ysw_ref[rows, :], wbs_ref[...]))

        def out_proj():
            merged = gate(0) * fill.get(("ml", k))
            merged = merged + gate(1) * fill.get(("sg", k))
            merged = merged + gate(2) * fill.get(("sw", k))
            return _dot(merged.astype(BF16), wout_ref[...])

        def emit():
            y = load_x(k) + _rms(fill.get(("o", k))) * (_rows(g1_ref[...], R, Ls) * gpost_ref[...])
            xo_ref[:, toks(k), :] = y.reshape(R, Ls, D_MODEL)
            return None

        fill.add(("o", k), out_proj)
        fill.add(("emit", k), emit)

    ri = lax.broadcasted_iota(jnp.int32, (Ls, Ls), 0)
    ci = lax.broadcasted_iota(jnp.int32, (Ls, Ls), 1)
    triu = jnp.where(ri <= ci, 1.0, 0.0).astype(BF16)
    triu3 = jnp.concatenate([triu] * 3, axis=0)
    gi = lax.broadcasted_iota(jnp.int32, (Lg, Lg), 0)
    gj = lax.broadcasted_iota(jnp.int32, (Lg, Lg), 1)
    n_q = SW_REP * SW_CHUNK
    rid = lax.broadcasted_iota(jnp.int32, (n_q, 1), 0) // SW_CHUNK
    sink_cols = []
    for g in range(SW_KV):
        col = jnp.full((n_q, 1), sinks_ref[SW_REP * g + SW_REP - 1], F32)
        for i in range(SW_REP - 2, -1, -1):
            col = jnp.where(rid == i, sinks_ref[SW_REP * g + i], col)
        sink_cols.append(col)
    kpos_chunk = lax.broadcasted_iota(jnp.int32, (n_q, BAND), 1) // SW_CHUNK

    def mixers(k):
        rows = slice(k * T, (k + 1) * T)
        grow = fill.get(("grow", k))
        lf_row = jax.nn.log_sigmoid(grow)
        mqb, mkt, mv = fill.get(("mq", k)), fill.get(("mkt", k)), fill.get(("mv", k))
        for r in range(R):
            rs = slice(r * Ls, (r + 1) * Ls)
            brow_all = _dot(jnp.concatenate(_split3(lf_row[:, rs]), axis=1), triu3)
            bcol_all = brow_all.T
            for hd in range(ML_HEADS):
                hs = slice(hd * ML_DH, (hd + 1) * ML_DH)
                fg = ML_HEADS + hd
                h_ml, caug_new, m_new = _mlstm_unit(
                    mqb[rs, hs], mkt[hs, rs], mv[rs, hs],
                    grow[hd:hd + 1, rs] - brow_all[fg:fg + 1, :], bcol_all[:, fg:fg + 1],
                    st_ref[r, hd], m_ref[r, hd:hd + 1, 0:1])
                st_ref[r, hd] = caug_new
                m_ref[r, hd:hd + 1, :] = jnp.broadcast_to(m_new, (1, LANES))
                fill.pump(w_ml)
                yml_ref[k * T + r * Ls:k * T + (r + 1) * Ls, hs] = (
                    fill.get(("mo", k))[rs, hs] * h_ml).astype(BF16)
        fill.add(("ml", k), lambda: _dot(yml_ref[rows, :], wbm_ref[...]))

        vnorm = fill.get(("vnorm", k))
        su = fill.get(("su", k))
        if has_state:
            gv_ref[...] = vnorm.reshape(R, Ls, SG_W)
        n_chunk = T // Lg
        for g in range(SG_GROUPS):
            cs = slice(g * SG_GDIM, (g + 1) * SG_GDIM)
            w = jnp.where(gi >= gj, wsp_ref[g, :Lg, :Lg], 0.0).astype(BF16)
            vcat = jnp.concatenate([vnorm[c * Lg:(c + 1) * Lg, cs] for c in range(n_chunk)], axis=1)
            z = _dot(w, vcat.astype(BF16))
            for c in range(n_chunk):
                ts = slice(c * Lg, (c + 1) * Lg)
                zc = z[:, c * SG_GDIM:(c + 1) * SG_GDIM] + bsp_ref[g, :Lg, :]
                ysg_ref[k * T + c * Lg:k * T + (c + 1) * Lg, cs] = (su[ts, cs] * zc).astype(BF16)
            fill.pump(w_sg)
        fill.add(("sg", k), lambda: _dot(ysg_ref[rows, :], wbg_ref[...]))

        wqb, wk, wv = fill.get(("wq", k)), fill.get(("wk", k)), fill.get(("wv", k))
        for r in range(R):
            if has_state:
                kband = jnp.concatenate([ck_ref[r], wk[r * Ls:(r + 1) * Ls]], axis=0)
                vband = jnp.concatenate([cv_ref[r], wv[r * Ls:(r + 1) * Ls]], axis=0)
            else:
                kband = jnp.concatenate([kprev_ref[...], wk], axis=0)
                vband = jnp.concatenate([vprev_ref[...], wv], axis=0)
            lane = lax.broadcasted_iota(jnp.int32, kband.shape, 1)
            k_group = (lane // SW_HALF) % SW_KV
            v_group = lane // SW_DH
            kg = [jnp.where(k_group == g, kband, 0.0).astype(BF16) for g in range(SW_KV)]
            vaug = [jnp.concatenate([jnp.where(v_group == g, vband, 0.0),
                                     jnp.where(v_group == g, 1.0, 0.0)], axis=1).astype(BF16)
                    for g in range(SW_KV)]
            for j in range(n_sw):
                t0 = r * Ls + j * SW_CHUNK
                qstack = jnp.concatenate([wqb[t0:t0 + SW_CHUNK, i * LANES:(i + 1) * LANES]
                                          for i in range(SW_REP)], axis=0)
                needs_mask = not has_state and k == 0 and j < WINDOW // SW_CHUNK
                mask = kpos_chunk >= (WINDOW // SW_CHUNK) - (seq * n_sub * n_sw + j) if needs_mask else None
                ks = slice(j * SW_CHUNK, j * SW_CHUNK + BAND)
                y = _swa_chunk(qstack, [a[ks] for a in kg], [a[ks] for a in vaug], sink_cols, mask)
                for i in range(SW_REP):
                    ysw_ref[k * T + t0:k * T + t0 + SW_CHUNK, i * LANES:(i + 1) * LANES] = (
                        y[i * SW_CHUNK:(i + 1) * SW_CHUNK].astype(BF16))
                fill.pump(w_sw)
        if has_state:
            kn_ref[...] = wk.reshape(R, Ls, SW_KVW)
            vn_ref[...] = wv.reshape(R, Ls, SW_KVW)
        else:
            kprev_ref[...] = wk[T - WINDOW:]
            vprev_ref[...] = wv[T - WINDOW:]

    fill.add(("hb", 0), lambda: norm(0))
    queue_head(0)
    for k in range(n_sub):
        for name in ("grow", "mq", "mkt", "mv"):
            fill.get((name, k))
        queue_body(k, with_gates=False)
        if k + 1 < n_sub:
            fill.add(("hb", k + 1), functools.partial(norm, k + 1))
            fill.get(("hb", k + 1))
            queue_head(k + 1)
        queue_body(k, with_gates=True)
        mixers(k)
        queue_tail(k)
    fill.drain()

    @pl.when(seq == n_seq - 1)
    def _emit_state():
        if not has_state:
            kn_ref[0] = kprev_ref[...]
            vn_ref[0] = vprev_ref[...]
        for r in range(R):
            for hd in range(ML_HEADS):
                cout_ref[r, hd] = st_ref[r, hd, :, :ML_DH]
                nout_ref[r, hd:hd + 1, :] = st_ref[r, hd, :, ML_DH:].T[0:1, :]


def _const_spec(shape, layer):
    nd = len(shape)
    return pl.BlockSpec((None,) + tuple(shape), lambda b, s: (layer,) + (0,) * nd,
                        pipeline_mode=pl.Buffered(1))


def _mixer(x, mod, layer, wts, cos_t, sin_t, state, prev, *, R, Lr, Lg, n_sub):
    B, S, _ = x.shape
    n_seq = S // Lr
    has_state = state is not None
    T = R * Lr
    assert n_sub == 1 or R == 1

    def mod_spec(j):
        return pl.BlockSpec((None, None, R, 1, D_MODEL), lambda b, s: (layer, j, b, 0, 0))

    in_specs = [
        pl.BlockSpec((R, Lr, D_MODEL), lambda b, s: (b, s, 0)),
        mod_spec(0), mod_spec(1), mod_spec(2),
        _const_spec((D_MODEL, D_IN_PAD), layer),
        _const_spec((ML_W + GATE_ROWS, D_MODEL), layer),
        _const_spec((1, D_MODEL), layer),
        _const_spec((1, D_MODEL), layer),
        _const_spec((GATE_ROWS, 1), layer),
        _const_spec((1, SG_W), layer),
        _const_spec((1, SG_W), layer),
        _const_spec((SG_GROUPS, SG_CHUNK, SG_CHUNK), layer),
        _const_spec((SG_GROUPS, SG_CHUNK, SG_GDIM), layer),
        pl.BlockSpec(memory_space=pltpu.SMEM),
        _const_spec((ML_W, D_MODEL), layer),
        _const_spec((SG_W, D_MODEL), layer),
        _const_spec((SW_W, D_MODEL), layer),
        _const_spec((D_MODEL, D_MODEL), layer),
        pl.BlockSpec((Lr, LANES), (lambda b, s: (0, 0)) if has_state else (lambda b, s: (s, 0))),
        pl.BlockSpec((Lr, LANES), (lambda b, s: (0, 0)) if has_state else (lambda b, s: (s, 0))),
    ]
    args = [x, mod, mod, mod, wts["win"], wts["wkt"], wts["g_pre_mix"], wts["g_post_mix"],
            wts["bg_row"], wts["ln_v_g"], wts["ln_v_b"], wts["w_spatial"], wts["b_spatial_b"],
            wts["swa_sinks"][layer], wts["w_br_mlstm"], wts["w_br_gmlp"], wts["w_br_swa"], wts["w_out"],
            cos_t, sin_t]
    n_new = Lr if has_state else WINDOW
    stacked = [
        ((DEPTH, B, ML_HEADS, ML_DH, ML_DH), (None, R, ML_HEADS, ML_DH, ML_DH), lambda b, s: (layer, b, 0, 0, 0)),
        ((DEPTH, B, ML_HEADS, ML_DH), (None, R, ML_HEADS, ML_DH), lambda b, s: (layer, b, 0, 0)),
        ((DEPTH, B, n_new, SW_KVW), (None, R, n_new, SW_KVW), lambda b, s: (layer, b, 0, 0)),
        ((DEPTH, B, n_new, SW_KVW), (None, R, n_new, SW_KVW), lambda b, s: (layer, b, 0, 0)),
    ]
    if has_state:
        stacked.append(((DEPTH, B, S, SG_W), (None, R, Lr, SG_W), lambda b, s: (layer, b, s, 0)))
    st_shapes = [jax.ShapeDtypeStruct(shp, F32) for shp, _, _ in stacked]
    st_specs = [pl.BlockSpec(blk, imap) for _, blk, imap in stacked]
    x_out = (jax.ShapeDtypeStruct((B, S, D_MODEL), F32), pl.BlockSpec((R, Lr, D_MODEL), lambda b, s: (b, s, 0)))
    m_out = (jax.ShapeDtypeStruct((B, ML_HEADS, LANES), F32),
             pl.BlockSpec((R, ML_HEADS, LANES), lambda b, s: (b, 0, 0)))
    out_shape = [x_out[0], st_shapes[0], st_shapes[1], m_out[0]] + st_shapes[2:]
    out_specs = [x_out[1], st_specs[0], st_specs[1], m_out[1]] + st_specs[2:]
    stacked_out_idx = [1, 2] + list(range(4, 4 + len(stacked) - 2))
    scratch = [pltpu.VMEM((R, ML_HEADS, ML_DH, 2 * ML_DH), F32),
               pltpu.VMEM((T, ML_W), BF16), pltpu.VMEM((T, SG_W), BF16), pltpu.VMEM((T, SW_W), BF16)]
    if has_state:
        c_in, n_in, m_in, cache_k, cache_v = state
        in_specs += [
            pl.BlockSpec((None, R, ML_HEADS, ML_DH, ML_DH), lambda b, s: (layer, b, 0, 0, 0)),
            pl.BlockSpec((None, R, ML_HEADS, ML_DH), lambda b, s: (layer, b, 0, 0)),
            pl.BlockSpec((None, R, ML_HEADS, LANES), lambda b, s: (layer, b, 0, 0)),
            pl.BlockSpec((None, R, WINDOW, SW_KVW), lambda b, s: (layer, b, 0, 0)),
            pl.BlockSpec((None, R, WINDOW, SW_KVW), lambda b, s: (layer, b, 0, 0)),
        ]
        args += [c_in, n_in, m_in, cache_k, cache_v]
    else:
        scratch += [pltpu.VMEM((WINDOW, SW_KVW), F32), pltpu.VMEM((WINDOW, SW_KVW), F32)]
    if prev is None:
        prev = [jnp.zeros(sds.shape, sds.dtype) for sds in st_shapes]
    aliases = {}
    for arr, oi in zip(prev, stacked_out_idx):
        aliases[len(args)] = oi
        in_specs.append(pl.BlockSpec(memory_space=pl.ANY))
        args.append(arr)

    outs = pl.pallas_call(
        functools.partial(_mixer_kernel, R=R, Lr=Lr, Lg=Lg, has_state=has_state, n_seq=n_seq,
                          n_alias=len(aliases), n_sub=n_sub),
        grid=(B // R, n_seq),
        in_specs=in_specs,
        out_specs=out_specs,
        out_shape=out_shape,
        scratch_shapes=scratch,
        input_output_aliases=aliases,
        compiler_params=pltpu.CompilerParams(
            dimension_semantics=("parallel", "arbitrary"), vmem_limit_bytes=VMEM_LIMIT),
        name="mixer_sample" if has_state else "mixer_prompt",
    )(*args)
    return outs[0], outs[3], [outs[i] for i in stacked_out_idx]


def _ffn_kernel(x_ref, sh_ref, sc_ref, g_ref, gpre_ref, gpost_ref, wfi_ref, wfo_ref, o_ref, *, R, Lr, n_sub):
    Rs = R // n_sub if R > 1 else 1
    Ls = Lr if R > 1 else Lr // n_sub
    Ts = Rs * Ls

    def rows_of(ref, k):
        return _rows(ref[k * Rs:(k + 1) * Rs] if R > 1 else ref[...], Rs, Ls)

    def load(k):
        xk = x_ref[k * Rs:(k + 1) * Rs] if R > 1 else x_ref[:, k * Ls:(k + 1) * Ls, :]
        return xk.reshape(Ts, D_MODEL)

    def prologue(k):
        x = load(k)
        scale = gpre_ref[...] * (1.0 + rows_of(sc_ref, k))
        return (_rms(x) * scale + rows_of(sh_ref, k)).astype(BF16)

    def matmuls(hb):
        f = None
        for lo, hi in FFN_SPLITS:
            gate = _dot(hb, wfi_ref[:, lo:hi])
            up = _dot(hb, wfi_ref[:, FFN_HIDDEN + lo:FFN_HIDDEN + hi])
            act = (gate * (jnp.tanh(gate) + 1.0) * up).astype(BF16)
            part = _dot(act, wfo_ref[lo:hi, :])
            f = part if f is None else f + part
        return f

    def epilogue(k, f):
        y = (load(k) + _rms(f) * (rows_of(g_ref, k) * gpost_ref[...])).reshape(Rs, Ls, D_MODEL)
        if R > 1:
            o_ref[k * Rs:(k + 1) * Rs] = y
        else:
            o_ref[:, k * Ls:(k + 1) * Ls, :] = y

    hb = prologue(0)
    for k in range(n_sub):
        hb_next = prologue(k + 1) if k + 1 < n_sub else None
        f = matmuls(hb)
        epilogue(k, f)
        hb = hb_next


def _ffn(x, mod, layer, wts, *, R, Lr, n_sub):
    B, S, _ = x.shape

    def mod_spec(j):
        return pl.BlockSpec((None, None, R, 1, D_MODEL), lambda b, s: (layer, j, b, 0, 0))

    return pl.pallas_call(
        functools.partial(_ffn_kernel, R=R, Lr=Lr, n_sub=n_sub),
        grid=(B // R, S // Lr),
        in_specs=[
            pl.BlockSpec((R, Lr, D_MODEL), lambda b, s: (b, s, 0)),
            mod_spec(3), mod_spec(4), mod_spec(5),
            _const_spec((1, D_MODEL), layer),
            _const_spec((1, D_MODEL), layer),
            _const_spec((D_MODEL, 2 * FFN_HIDDEN), layer),
            _const_spec((FFN_HIDDEN, D_MODEL), layer),
        ],
        out_specs=pl.BlockSpec((R, Lr, D_MODEL), lambda b, s: (b, s, 0)),
        out_shape=jax.ShapeDtypeStruct((B, S, D_MODEL), F32),
        compiler_params=pltpu.CompilerParams(
            dimension_semantics=("parallel", "parallel"), vmem_limit_bytes=VMEM_LIMIT),
        name="ffn",
    )(x, mod, mod, mod, wts["g_pre_ffn"], wts["g_post_ffn"], wts["w_ffn_in"], wts["w_ffn_out"])


def _pair_layout(a):
    lead = a.shape[:-1]
    return jnp.swapaxes(a.reshape(lead + (2, 2, SW_HALF)), -3, -2).reshape(lead + (2 * SW_DH,))


def _query_layout(wq):
    lead = wq.shape[:-1]
    w = wq.reshape(lead + (SW_KV, SW_REP, 2, SW_HALF))
    return jnp.moveaxis(w, -4, -2).reshape(lead + (SW_W,))


def _rope_tables(pos):
    inv = ROPE_THETA ** (-jnp.arange(SW_HALF, dtype=F32) / SW_HALF)
    ang = pos.astype(F32)[:, None] * inv[None, :]
    cos, sin = jnp.cos(ang), jnp.sin(ang)
    cos_t = jnp.concatenate([cos, cos, cos, cos], axis=1)
    sin_t = jnp.concatenate([-sin, -sin, sin, sin], axis=1)
    return cos_t, sin_t


def _prep_weights(w_in, b_igate, b_fgate, g_pre_mix, g_post_mix, g_pre_ffn, g_post_ffn, ln_v_g, ln_v_b,
                  w_spatial, b_spatial, swa_sinks, w_br_mlstm, w_br_gmlp, w_br_swa, w_out, w_ffn_in, w_ffn_out):
    offs = np.cumsum((0,) + SPLIT_SIZES)
    seg = [w_in[..., offs[i]:offs[i + 1]] for i in range(len(SPLIT_SIZES))]
    mq, mk, mv, mo, mi, mf, su, sv, wq, wk, wv, gl = seg
    gates = jnp.concatenate([mi, mf], axis=-1)
    win = jnp.concatenate([a.astype(BF16) for a in
                           (mq, mv, 0.5 * mo, su, sv, _query_layout(wq), _pair_layout(wk), wv, 0.5 * gl)], axis=-1)
    w_ffn_in = jnp.concatenate([0.5 * w_ffn_in[..., :FFN_HIDDEN], w_ffn_in[..., FFN_HIDDEN:]], axis=-1)
    bg = jnp.concatenate([b_igate, b_fgate], axis=-1)
    row = lambda a: a.reshape(DEPTH, 1, a.shape[-1])
    wbs = jnp.swapaxes(w_br_swa.reshape(DEPTH, SW_KV, SW_REP, SW_DH, D_MODEL), 1, 2).reshape(DEPTH, SW_W, D_MODEL)
    return dict(
        win=win,
        wkt=jnp.concatenate([jnp.swapaxes(mk, 1, 2),
                             jnp.pad(jnp.swapaxes(gates, 1, 2), ((0, 0), (0, GATE_ROWS - N_GATE), (0, 0)))],
                            axis=1).astype(BF16),
        bg_row=jnp.pad(bg, ((0, 0), (0, GATE_ROWS - N_GATE))).reshape(DEPTH, GATE_ROWS, 1),
        g_pre_mix=row(g_pre_mix), g_post_mix=row(g_post_mix),
        g_pre_ffn=row(g_pre_ffn), g_post_ffn=row(g_post_ffn),
        ln_v_g=row(ln_v_g), ln_v_b=row(ln_v_b),
        w_spatial=w_spatial,
        b_spatial_b=jnp.broadcast_to(b_spatial[..., None], b_spatial.shape + (SG_GDIM,)),
        swa_sinks=swa_sinks,
        w_br_mlstm=w_br_mlstm.astype(BF16), w_br_gmlp=w_br_gmlp.astype(BF16),
        w_br_swa=wbs.astype(BF16), w_out=w_out.astype(BF16),
        w_ffn_in=w_ffn_in.astype(BF16), w_ffn_out=w_ffn_out.astype(BF16),
    )


MIX_TOKENS_PROMPT = 512
MIX_SUB_PROMPT = 2
MIX_ROWS_SAMPLE = 4
FFN_TOKENS = 1024
FFN_SUB = 2


def kernel(x_prompt, x_sample, c_prompt, c_sample, state_mlstm_C, state_mlstm_n, state_mlstm_m, cache_swa_k, cache_swa_v, w_ada, b_ada, g_pre_mix, g_post_mix, g_pre_ffn, g_post_ffn, w_in, b_igate, b_fgate, ln_v_g, ln_v_b, w_spatial, b_spatial, swa_sinks, w_br_mlstm, w_br_gmlp, w_br_swa, w_out, w_ffn_in, w_ffn_out):
    Bp, Sp, _ = x_prompt.shape
    Bs, Ss, _ = x_sample.shape
    wts = _prep_weights(w_in, b_igate, b_fgate, g_pre_mix, g_post_mix, g_pre_ffn, g_post_ffn, ln_v_g, ln_v_b,
                        w_spatial, b_spatial, swa_sinks, w_br_mlstm, w_br_gmlp, w_br_swa, w_out,
                        w_ffn_in, w_ffn_out)
    mod = _modulation(jnp.concatenate([c_prompt, c_sample], axis=0), w_ada, b_ada)
    mod = mod.reshape(DEPTH, 6, Bp + Bs, 1, D_MODEL)
    mod_p, mod_s = mod[:, :, :Bp], mod[:, :, Bp:]
    cos_p, sin_p = _rope_tables(jnp.arange(Sp))
    cos_s, sin_s = _rope_tables(PAST_LEN + jnp.arange(Ss))

    m_in = jnp.broadcast_to(state_mlstm_m[..., None], state_mlstm_m.shape + (LANES,))
    ck = _pair_layout(cache_swa_k.reshape(DEPTH, Bs, WINDOW, SW_KVW))
    cv = cache_swa_v.reshape(DEPTH, Bs, WINDOW, SW_KVW)
    state = (state_mlstm_C, state_mlstm_n, m_in, ck, cv)

    xp, xs = x_prompt, x_sample
    st_p, st_s, ms_p, ms_s = None, None, [], []
    for l in range(DEPTH):
        xp, m_p, st_p = _mixer(xp, mod_p, l, wts, cos_p, sin_p, None, st_p,
                               R=1, Lr=MIX_TOKENS_PROMPT, Lg=SG_CHUNK, n_sub=MIX_SUB_PROMPT)
        xp = _ffn(xp, mod_p, l, wts, R=1, Lr=FFN_TOKENS, n_sub=FFN_SUB)
        xs, m_s, st_s = _mixer(xs, mod_s, l, wts, cos_s, sin_s, state, st_s,
                               R=MIX_ROWS_SAMPLE, Lr=Ss, Lg=Ss, n_sub=1)
        xs = _ffn(xs, mod_s, l, wts, R=FFN_TOKENS // Ss, Lr=Ss, n_sub=FFN_SUB)
        ms_p.append(m_p)
        ms_s.append(m_s)

    def unpack(st, ms, B, n_new):
        C, n, k, v = st[:4]
        m = jnp.stack(ms)[..., 0]
        k = _pair_layout(k).reshape(DEPTH, B, n_new, SW_KV, SW_DH)
        return C, n, m, k, v.reshape(DEPTH, B, n_new, SW_KV, SW_DH)

    C_p, n_p, m_p, k_p, v_p = unpack(st_p, ms_p, Bp, WINDOW)
    C_s, n_s, m_s, k_s, v_s = unpack(st_s, ms_s, Bs, Ss)
    return (xp, xs, C_p, n_p, m_p, k_p, v_p, C_s, n_s, m_s, k_s, v_s, st_s[4])
```

```python
import functools

import numpy as np
import jax
import jax.numpy as jnp
from jax import lax
from jax.experimental import pallas as pl
from jax.experimental.pallas import tpu as pltpu

F32 = jnp.float32
BF16 = jnp.bfloat16

D_MODEL = 1024
DEPTH = 4
PAST_LEN = 1024
ML_HEADS = 4
ML_DH = 128
ML_W = ML_HEADS * ML_DH
SG_GROUPS = 4
SG_CHUNK = 128
SG_W = 512
SG_GDIM = SG_W // SG_GROUPS
SW_HEADS = 8
SW_KV = 2
SW_REP = SW_HEADS // SW_KV
SW_DH = 64
SW_HALF = SW_DH // 2
SW_W = SW_HEADS * SW_DH
SW_KVW = SW_KV * SW_DH
SW_CHUNK = 64
WINDOW = 128
BAND = WINDOW + SW_CHUNK
ROPE_THETA = 10000.0
FFN_HIDDEN = 2816
EPS = 1e-6
SPLIT_SIZES = (ML_W, ML_W, ML_W, ML_W, ML_HEADS, ML_HEADS, SG_W, SG_W, SW_W, SW_KVW, SW_KVW, 3 * D_MODEL)

LANES = 128
N_GATE = 2 * ML_HEADS
GATE_ROWS = 16

OFF_MQ = 0
OFF_MV = OFF_MQ + ML_W
OFF_MO = OFF_MV + ML_W
OFF_SU = OFF_MO + ML_W
OFF_SV = OFF_SU + SG_W
OFF_WQ = OFF_SV + SG_W
OFF_WK = OFF_WQ + SW_W
OFF_WV = OFF_WK + SW_KVW
OFF_GL = OFF_WV + SW_KVW
D_IN_PAD = OFF_GL + 3 * D_MODEL

FFN_SPLITS = ((0, 1536), (1536, FFN_HIDDEN))

VMEM_LIMIT = 56 * 1024 * 1024

NT_DIMS = (((1,), (1,)), ((), ()))


def _dot(a, b):
    return jnp.dot(a, b, preferred_element_type=F32)


def _dot_nt(a, b):
    return lax.dot_general(a, b, NT_DIMS, preferred_element_type=F32)


def _sigmoid(x):
    return 0.5 * jnp.tanh(0.5 * x) + 0.5


def _sigmoid_of_half(z):
    return 0.5 * jnp.tanh(z) + 0.5


def _rms(x):
    return x * lax.rsqrt(jnp.mean(x * x, axis=-1, keepdims=True) + EPS)


def _rows(m3, R, Lr):
    W = m3.shape[-1]
    if R == 1:
        return m3.reshape(1, W)
    return jnp.broadcast_to(m3, (R, Lr, W)).reshape(R * Lr, W)


def _split3(x):
    h1 = x.astype(BF16)
    r1 = x - h1.astype(F32)
    h2 = r1.astype(BF16)
    h3 = (r1 - h2.astype(F32)).astype(BF16)
    return h1, h2, h3


class _Fillers:
    def __init__(self, total_weight):
        self._thunks, self._vals, self._left = {}, {}, total_weight

    def add(self, name, thunk):
        self._thunks[name] = thunk

    def get(self, name):
        if name not in self._vals:
            self._vals[name] = self._thunks.pop(name)()
        return self._vals[name]

    def pump(self, weight):
        n = -(-len(self._thunks) * weight // max(self._left, 1))
        self._left -= weight
        for name in list(self._thunks)[:n]:
            self.get(name)

    def drain(self):
        for name in list(self._thunks):
            if name in self._thunks:
                self.get(name)


def _mod_kernel(c_ref, w_ref, b_ref, o_ref):
    c = c_ref[...]
    a = (c * _sigmoid(c)).astype(BF16)
    o_ref[...] = _dot(a, w_ref[...].astype(BF16)) + b_ref[...]


def _modulation(c_all, w_ada, b_ada):
    nb = c_all.shape[0]
    return pl.pallas_call(
        _mod_kernel,
        grid=(DEPTH, 6),
        in_specs=[
            pl.BlockSpec((nb, D_MODEL), lambda l, j: (0, 0)),
            pl.BlockSpec((None, D_MODEL, D_MODEL), lambda l, j: (l, 0, j)),
            pl.BlockSpec((None, None, 1, D_MODEL), lambda l, j: (l, j, 0, 0)),
        ],
        out_specs=pl.BlockSpec((None, None, nb, D_MODEL), lambda l, j: (l, j, 0, 0)),
        out_shape=jax.ShapeDtypeStruct((DEPTH, 6, nb, D_MODEL), F32),
        compiler_params=pltpu.CompilerParams(dimension_semantics=("parallel", "parallel")),
        name="adaln_modulation",
    )(c_all, w_ada, b_ada.reshape(DEPTH, 6, 1, D_MODEL))


def _rope(x, cos, sin_signed):
    T, W = x.shape
    outs = []
    for j in range(W // LANES):
        xb = x[:, j * LANES:(j + 1) * LANES]
        outs.append(xb * cos + pltpu.roll(xb, LANES // 2, 1) * sin_signed)
    return outs[0] if len(outs) == 1 else jnp.concatenate(outs, axis=1)


def _mlstm_unit(qb, kt, v, a_row, b_col, caug, m_prev):
    L = qb.shape[0]
    ti = lax.broadcasted_iota(jnp.int32, (L, L), 0)
    si = lax.broadcasted_iota(jnp.int32, (L, L), 1)
    a_tri = jnp.where(ti >= si, a_row, -jnp.inf)
    g = jnp.maximum(jnp.max(a_tri, axis=-1, keepdims=True), m_prev)
    w_intra = jnp.exp(a_tri - g)
    w_inter = jnp.exp(m_prev - g)
    s = _dot(qb, kt.astype(BF16)) * w_intra
    vaug = jnp.concatenate([v, jnp.ones_like(v)], axis=1).astype(BF16)
    qc = _dot(qb, caug.astype(BF16))
    sv = _dot(s.astype(BF16), vaug)
    num = w_inter * qc[:, :ML_DH] + sv[:, :ML_DH]
    den = w_inter * qc[:, ML_DH:] + sv[:, ML_DH:]
    floor = jnp.exp(-(jnp.broadcast_to(b_col, (L, ML_DH)) + g))
    h = num / jnp.maximum(jnp.abs(den), floor)
    g_last = g[L - 1:L, :]
    ws_row = jnp.exp(a_row - g_last)
    upd = _dot((kt * ws_row).astype(BF16), vaug)
    caug_new = jnp.exp(m_prev - g_last) * caug + upd
    return h, caug_new, b_col[L - 1:L, :] + g_last


def _swa_chunk(qstack, kg, vaug, sink_cols, mask):
    o = None
    e_cols = []
    for g in range(SW_KV):
        s = _dot_nt(qstack, kg[g])
        if mask is not None:
            s = jnp.where(mask, s, -jnp.inf)
        mx = jnp.maximum(jnp.max(s, axis=-1, keepdims=True), sink_cols[g])
        p = jnp.exp(s - mx).astype(BF16)
        e_cols.append(jnp.exp(sink_cols[g] - mx))
        og = _dot(p, vaug[g])
        o = og if o is None else o + og
    lane = lax.broadcasted_iota(jnp.int32, (qstack.shape[0], LANES), 1)
    e = jnp.where(lane < SW_DH, e_cols[0], e_cols[1])
    return o[:, :LANES] / (o[:, LANES:] + e)


def _mixer_kernel(*refs, R, Lr, Lg, has_state, n_seq, n_alias, n_sub):
    it = iter(refs)
    x_ref, sh1_ref, sc1_ref, g1_ref = next(it), next(it), next(it), next(it)
    win_ref, wkt_ref, gpre_ref, gpost_ref = next(it), next(it), next(it), next(it)
    bgr_ref = next(it)
    lng_ref, lnb_ref, wsp_ref, bsp_ref = next(it), next(it), next(it), next(it)
    sinks_ref = next(it)
    wbm_ref, wbg_ref, wbs_ref, wout_ref = next(it), next(it), next(it), next(it)
    cos_ref, sin_ref = next(it), next(it)
    if has_state:
        cin_ref, nin_ref, min_ref, ck_ref, cv_ref = next(it), next(it), next(it), next(it), next(it)
    for _ in range(n_alias):
        next(it)
    xo_ref, cout_ref, nout_ref, m_ref, kn_ref, vn_ref = (next(it), next(it), next(it), next(it), next(it),
                                                         next(it))
    if has_state:
        gv_ref = next(it)
    st_ref, yml_ref, ysg_ref, ysw_ref = next(it), next(it), next(it), next(it)
    if not has_state:
        kprev_ref, vprev_ref = next(it), next(it)

    Ls = Lr // n_sub
    T = R * Ls
    seq = pl.program_id(1)

    @pl.when(seq == 0)
    def _init():
        if has_state:
            for r in range(R):
                for hd in range(ML_HEADS):
                    st_ref[r, hd, :, :ML_DH] = cin_ref[r, hd]
                    st_ref[r, hd, :, ML_DH:] = jnp.broadcast_to(nin_ref[r, hd:hd + 1, :], (ML_DH, ML_DH)).T
            m_ref[...] = min_ref[...]
        else:
            st_ref[...] = jnp.zeros_like(st_ref)
            m_ref[...] = jnp.zeros_like(m_ref)
            kprev_ref[...] = jnp.zeros_like(kprev_ref)
            vprev_ref[...] = jnp.zeros_like(vprev_ref)

    n_sw = Ls // SW_CHUNK
    w_ml, w_sg, w_sw = (3, 1, 3) if Ls > SW_CHUNK else (1, 1, 1)
    fill = _Fillers(n_sub * (R * ML_HEADS * w_ml + SG_GROUPS * w_sg + R * n_sw * w_sw))

    def toks(k):
        return slice(k * Ls, (k + 1) * Ls)

    def load_x(k):
        return x_ref[:, toks(k), :].reshape(T, D_MODEL)

    def norm(k):
        scale = gpre_ref[...] * (1.0 + _rows(sc1_ref[...], R, Ls))
        return (_rms(load_x(k)) * scale + _rows(sh1_ref[...], R, Ls)).astype(BF16)

    def queue_head(k):
        hb = lambda: fill.get(("hb", k))
        fill.add(("kg", k), lambda: _dot_nt(wkt_ref[...], hb()))
        fill.add(("grow", k), lambda: fill.get(("kg", k))[ML_W:, :] + bgr_ref[...])
        fill.add(("mq", k), lambda: _dot(hb(), win_ref[:, OFF_MQ:OFF_MQ + ML_W]).astype(BF16))
        fill.add(("mkt", k), lambda: fill.get(("kg", k))[:ML_W, :] * (ML_DH ** -0.5))
        fill.add(("mv", k), lambda: _dot(hb(), win_ref[:, OFF_MV:OFF_MV + ML_W]))

    def queue_body(k, with_gates):
        hb = lambda: fill.get(("hb", k))

        def proj(lo, width):
            return _dot(hb(), win_ref[:, lo:lo + width])

        def gmlp_inputs():
            sv = proj(OFF_SV, SG_W)
            mu = jnp.mean(sv, axis=-1, keepdims=True)
            svc = sv - mu
            var = jnp.mean(svc * svc, axis=-1, keepdims=True)
            return svc * lax.rsqrt(var + EPS) * lng_ref[...] + lnb_ref[...]

        def rope_tables():
            cos, sin = cos_ref[toks(k), :], sin_ref[toks(k), :]
            if R > 1:
                cos = jnp.concatenate([cos] * R, axis=0)
                sin = jnp.concatenate([sin] * R, axis=0)
            return cos, sin

        if not with_gates:
            fill.add(("mo", k), lambda: _sigmoid_of_half(proj(OFF_MO, ML_W)))
            fill.add(("su", k), lambda: proj(OFF_SU, SG_W))
            fill.add(("vnorm", k), gmlp_inputs)
            fill.add(("wq", k), lambda: (_rope(proj(OFF_WQ, SW_W), *rope_tables()) * (SW_DH ** -0.5)).astype(BF16))
            fill.add(("wk", k), lambda: _rope(proj(OFF_WK, SW_KVW), *rope_tables()))
            fill.add(("wv", k), lambda: proj(OFF_WV, SW_KVW))
        else:
            for i in range(3):
                fill.add(("gl", k, i), functools.partial(
                    lambda i: _sigmoid_of_half(proj(OFF_GL + i * D_MODEL, D_MODEL)), i))

    def queue_tail(k):
        rows = slice(k * T, (k + 1) * T)
        gate = lambda b: fill.get(("gl", k, b))
        fill.add(("sw", k), lambda: _dot(ysw_ref[rows, :], wbs_ref[...]))

        def out_proj():
            merged = gate(0) * fill.get(("ml", k))
            merged = merged + gate(1) * fill.get(("sg", k))
            merged = merged + gate(2) * fill.get(("sw", k))
            return _dot(merged.astype(BF16), wout_ref[...])

        def emit():
            y = load_x(k) + _rms(fill.get(("o", k))) * (_rows(g1_ref[...], R, Ls) * gpost_ref[...])
            xo_ref[:, toks(k), :] = y.reshape(R, Ls, D_MODEL)
            return None

        fill.add(("o", k), out_proj)
        fill.add(("emit", k), emit)

    ri = lax.broadcasted_iota(jnp.int32, (Ls, Ls), 0)
    ci = lax.broadcasted_iota(jnp.int32, (Ls, Ls), 1)
    triu = jnp.where(ri <= ci, 1.0, 0.0).astype(BF16)
    triu3 = jnp.concatenate([triu] * 3, axis=0)
    gi = lax.broadcasted_iota(jnp.int32, (Lg, Lg), 0)
    gj = lax.broadcasted_iota(jnp.int32, (Lg, Lg), 1)
    n_q = SW_REP * SW_CHUNK
    rid = lax.broadcasted_iota(jnp.int32, (n_q, 1), 0) // SW_CHUNK
    sink_cols = []
    for g in range(SW_KV):
        col = jnp.full((n_q, 1), sinks_ref[SW_REP * g + SW_REP - 1], F32)
        for i in range(SW_REP - 2, -1, -1):
            col = jnp.where(rid == i, sinks_ref[SW_REP * g + i], col)
        sink_cols.append(col)
    kpos_chunk = lax.broadcasted_iota(jnp.int32, (n_q, BAND), 1) // SW_CHUNK

    def mixers(k):
        rows = slice(k * T, (k + 1) * T)
        grow = fill.get(("grow", k))
        lf_row = jax.nn.log_sigmoid(grow)
        mqb, mkt, mv = fill.get(("mq", k)), fill.get(("mkt", k)), fill.get(("mv", k))
        for r in range(R):
            rs = slice(r * Ls, (r + 1) * Ls)
            brow_all = _dot(jnp.concatenate(_split3(lf_row[:, rs]), axis=1), triu3)
            bcol_all = brow_all.T
            for hd in range(ML_HEADS):
                hs = slice(hd * ML_DH, (hd + 1) * ML_DH)
                fg = ML_HEADS + hd
                fill.pump(w_ml)
                h_ml, caug_new, m_new = _mlstm_unit(
                    mqb[rs, hs], mkt[hs, rs], mv[rs, hs],
                    grow[hd:hd + 1, rs] - brow_all[fg:fg + 1, :], bcol_all[:, fg:fg + 1],
                    st_ref[r, hd], m_ref[r, hd:hd + 1, 0:1])
                st_ref[r, hd] = caug_new
                m_ref[r, hd:hd + 1, :] = jnp.broadcast_to(m_new, (1, LANES))
                yml_ref[k * T + r * Ls:k * T + (r + 1) * Ls, hs] = (
                    fill.get(("mo", k))[rs, hs] * h_ml).astype(BF16)
        fill.add(("ml", k), lambda: _dot(yml_ref[rows, :], wbm_ref[...]))

        vnorm = fill.get(("vnorm", k))
        su = fill.get(("su", k))
        if has_state:
            gv_ref[...] = vnorm.reshape(R, Ls, SG_W)
        n_chunk = T // Lg
        for g in range(SG_GROUPS):
            cs = slice(g * SG_GDIM, (g + 1) * SG_GDIM)
            fill.pump(w_sg)
            w = jnp.where(gi >= gj, wsp_ref[g, :Lg, :Lg], 0.0).astype(BF16)
            vcat = jnp.concatenate([vnorm[c * Lg:(c + 1) * Lg, cs] for c in range(n_chunk)], axis=1)
            z = _dot(w, vcat.astype(BF16))
            for c in range(n_chunk):
                ts = slice(c * Lg, (c + 1) * Lg)
                zc = z[:, c * SG_GDIM:(c + 1) * SG_GDIM] + bsp_ref[g, :Lg, :]
                ysg_ref[k * T + c * Lg:k * T + (c + 1) * Lg, cs] = (su[ts, cs] * zc).astype(BF16)
        fill.add(("sg", k), lambda: _dot(ysg_ref[rows, :], wbg_ref[...]))

        wqb, wk, wv = fill.get(("wq", k)), fill.get(("wk", k)), fill.get(("wv", k))
        for r in range(R):
            if has_state:
                kband = jnp.concatenate([ck_ref[r], wk[r * Ls:(r + 1) * Ls]], axis=0)
                vband = jnp.concatenate([cv_ref[r], wv[r * Ls:(r + 1) * Ls]], axis=0)
            else:
                kband = jnp.concatenate([kprev_ref[...], wk], axis=0)
                vband = jnp.concatenate([vprev_ref[...], wv], axis=0)
            lane = lax.broadcasted_iota(jnp.int32, kband.shape, 1)
            k_group = (lane // SW_HALF) % SW_KV
            v_group = lane // SW_DH
            kg = [jnp.where(k_group == g, kband, 0.0).astype(BF16) for g in range(SW_KV)]
            vaug = [jnp.concatenate([jnp.where(v_group == g, vband, 0.0),
                                     jnp.where(v_group == g, 1.0, 0.0)], axis=1).astype(BF16)
                    for g in range(SW_KV)]
            for j in range(n_sw):
                t0 = r * Ls + j * SW_CHUNK
                fill.pump(w_sw)
                qstack = jnp.concatenate([wqb[t0:t0 + SW_CHUNK, i * LANES:(i + 1) * LANES]
                                          for i in range(SW_REP)], axis=0)
                needs_mask = not has_state and k == 0 and j < WINDOW // SW_CHUNK
                mask = kpos_chunk >= (WINDOW // SW_CHUNK) - (seq * n_sub * n_sw + j) if needs_mask else None
                ks = slice(j * SW_CHUNK, j * SW_CHUNK + BAND)
                y = _swa_chunk(qstack, [a[ks] for a in kg], [a[ks] for a in vaug], sink_cols, mask)
                for i in range(SW_REP):
                    ysw_ref[k * T + t0:k * T + t0 + SW_CHUNK, i * LANES:(i + 1) * LANES] = (
                        y[i * SW_CHUNK:(i + 1) * SW_CHUNK].astype(BF16))
        if has_state:
            kn_ref[...] = wk.reshape(R, Ls, SW_KVW)
            vn_ref[...] = wv.reshape(R, Ls, SW_KVW)
        else:
            kprev_ref[...] = wk[T - WINDOW:]
            vprev_ref[...] = wv[T - WINDOW:]

    fill.add(("hb", 0), lambda: norm(0))
    queue_head(0)
    for k in range(n_sub):
        for name in ("grow", "mq", "mkt", "mv"):
            fill.get((name, k))
        queue_body(k, with_gates=False)
        if k + 1 < n_sub:
            fill.add(("hb", k + 1), functools.partial(norm, k + 1))
            fill.get(("hb", k + 1))
            queue_head(k + 1)
        queue_body(k, with_gates=True)
        mixers(k)
        queue_tail(k)
    fill.drain()

    @pl.when(seq == n_seq - 1)
    def _emit_state():
        if not has_state:
            kn_ref[0] = kprev_ref[...]
            vn_ref[0] = vprev_ref[...]
        for r in range(R):
            for hd in range(ML_HEADS):
                cout_ref[r, hd] = st_ref[r, hd, :, :ML_DH]
                nout_ref[r, hd:hd + 1, :] = st_ref[r, hd, :, ML_DH:].T[0:1, :]


def _const_spec(shape, layer):
    nd = len(shape)
    return pl.BlockSpec((None,) + tuple(shape), lambda b, s: (layer,) + (0,) * nd,
                        pipeline_mode=pl.Buffered(1))


def _mixer(x, mod, layer, wts, cos_t, sin_t, state, prev, *, R, Lr, Lg, n_sub):
    B, S, _ = x.shape
    n_seq = S // Lr
    has_state = state is not None
    T = R * Lr
    assert n_sub == 1 or R == 1

    def mod_spec(j):
        return pl.BlockSpec((None, None, R, 1, D_MODEL), lambda b, s: (layer, j, b, 0, 0))

    in_specs = [
        pl.BlockSpec((R, Lr, D_MODEL), lambda b, s: (b, s, 0)),
        mod_spec(0), mod_spec(1), mod_spec(2),
        _const_spec((D_MODEL, D_IN_PAD), layer),
        _const_spec((ML_W + GATE_ROWS, D_MODEL), layer),
        _const_spec((1, D_MODEL), layer),
        _const_spec((1, D_MODEL), layer),
        _const_spec((GATE_ROWS, 1), layer),
        _const_spec((1, SG_W), layer),
        _const_spec((1, SG_W), layer),
        _const_spec((SG_GROUPS, SG_CHUNK, SG_CHUNK), layer),
        _const_spec((SG_GROUPS, SG_CHUNK, SG_GDIM), layer),
        pl.BlockSpec(memory_space=pltpu.SMEM),
        _const_spec((ML_W, D_MODEL), layer),
        _const_spec((SG_W, D_MODEL), layer),
        _const_spec((SW_W, D_MODEL), layer),
        _const_spec((D_MODEL, D_MODEL), layer),
        pl.BlockSpec((Lr, LANES), (lambda b, s: (0, 0)) if has_state else (lambda b, s: (s, 0))),
        pl.BlockSpec((Lr, LANES), (lambda b, s: (0, 0)) if has_state else (lambda b, s: (s, 0))),
    ]
    args = [x, mod, mod, mod, wts["win"], wts["wkt"], wts["g_pre_mix"], wts["g_post_mix"],
            wts["bg_row"], wts["ln_v_g"], wts["ln_v_b"], wts["w_spatial"], wts["b_spatial_b"],
            wts["swa_sinks"][layer], wts["w_br_mlstm"], wts["w_br_gmlp"], wts["w_br_swa"], wts["w_out"],
            cos_t, sin_t]
    n_new = Lr if has_state else WINDOW
    stacked = [
        ((DEPTH, B, ML_HEADS, ML_DH, ML_DH), (None, R, ML_HEADS, ML_DH, ML_DH), lambda b, s: (layer, b, 0, 0, 0)),
        ((DEPTH, B, ML_HEADS, ML_DH), (None, R, ML_HEADS, ML_DH), lambda b, s: (layer, b, 0, 0)),
        ((DEPTH, B, n_new, SW_KVW), (None, R, n_new, SW_KVW), lambda b, s: (layer, b, 0, 0)),
        ((DEPTH, B, n_new, SW_KVW), (None, R, n_new, SW_KVW), lambda b, s: (layer, b, 0, 0)),
    ]
    if has_state:
        stacked.append(((DEPTH, B, S, SG_W), (None, R, Lr, SG_W), lambda b, s: (layer, b, s, 0)))
    st_shapes = [jax.ShapeDtypeStruct(shp, F32) for shp, _, _ in stacked]
    st_specs = [pl.BlockSpec(blk, imap) for _, blk, imap in stacked]
    x_out = (jax.ShapeDtypeStruct((B, S, D_MODEL), F32), pl.BlockSpec((R, Lr, D_MODEL), lambda b, s: (b, s, 0)))
    m_out = (jax.ShapeDtypeStruct((B, ML_HEADS, LANES), F32),
             pl.BlockSpec((R, ML_HEADS, LANES), lambda b, s: (b, 0, 0)))
    out_shape = [x_out[0], st_shapes[0], st_shapes[1], m_out[0]] + st_shapes[2:]
    out_specs = [x_out[1], st_specs[0], st_specs[1], m_out[1]] + st_specs[2:]
    stacked_out_idx = [1, 2] + list(range(4, 4 + len(stacked) - 2))
    scratch = [pltpu.VMEM((R, ML_HEADS, ML_DH, 2 * ML_DH), F32),
               pltpu.VMEM((T, ML_W), BF16), pltpu.VMEM((T, SG_W), BF16), pltpu.VMEM((T, SW_W), BF16)]
    if has_state:
        c_in, n_in, m_in, cache_k, cache_v = state
        in_specs += [
            pl.BlockSpec((None, R, ML_HEADS, ML_DH, ML_DH), lambda b, s: (layer, b, 0, 0, 0)),
            pl.BlockSpec((None, R, ML_HEADS, ML_DH), lambda b, s: (layer, b, 0, 0)),
            pl.BlockSpec((None, R, ML_HEADS, LANES), lambda b, s: (layer, b, 0, 0)),
            pl.BlockSpec((None, R, WINDOW, SW_KVW), lambda b, s: (layer, b, 0, 0)),
            pl.BlockSpec((None, R, WINDOW, SW_KVW), lambda b, s: (layer, b, 0, 0)),
        ]
        args += [c_in, n_in, m_in, cache_k, cache_v]
    else:
        scratch += [pltpu.VMEM((WINDOW, SW_KVW), F32), pltpu.VMEM((WINDOW, SW_KVW), F32)]
    if prev is None:
        prev = [jnp.zeros(sds.shape, sds.dtype) for sds in st_shapes]
    aliases = {}
    for arr, oi in zip(prev, stacked_out_idx):
        aliases[len(args)] = oi
        in_specs.append(pl.BlockSpec(memory_space=pl.ANY))
        args.append(arr)

    outs = pl.pallas_call(
        functools.partial(_mixer_kernel, R=R, Lr=Lr, Lg=Lg, has_state=has_state, n_seq=n_seq,
                          n_alias=len(aliases), n_sub=n_sub),
        grid=(B // R, n_seq),
        in_specs=in_specs,
        out_specs=out_specs,
        out_shape=out_shape,
        scratch_shapes=scratch,
        input_output_aliases=aliases,
        compiler_params=pltpu.CompilerParams(
            dimension_semantics=("parallel", "arbitrary"), vmem_limit_bytes=VMEM_LIMIT),
        name="mixer_sample" if has_state else "mixer_prompt",
    )(*args)
    return outs[0], outs[3], [outs[i] for i in stacked_out_idx]


def _ffn_kernel(x_ref, sh_ref, sc_ref, g_ref, gpre_ref, gpost_ref, wfi_ref, wfo_ref, o_ref, *, R, Lr, n_sub):
    Rs = R // n_sub if R > 1 else 1
    Ls = Lr if R > 1 else Lr // n_sub
    Ts = Rs * Ls

    def rows_of(ref, k):
        return _rows(ref[k * Rs:(k + 1) * Rs] if R > 1 else ref[...], Rs, Ls)

    def load(k):
        xk = x_ref[k * Rs:(k + 1) * Rs] if R > 1 else x_ref[:, k * Ls:(k + 1) * Ls, :]
        return xk.reshape(Ts, D_MODEL)

    def prologue(k):
        x = load(k)
        scale = gpre_ref[...] * (1.0 + rows_of(sc_ref, k))
        return (_rms(x) * scale + rows_of(sh_ref, k)).astype(BF16)

    def matmuls(hb):
        f = None
        for lo, hi in FFN_SPLITS:
            gate = _dot(hb, wfi_ref[:, lo:hi])
            up = _dot(hb, wfi_ref[:, FFN_HIDDEN + lo:FFN_HIDDEN + hi])
            act = (gate * (jnp.tanh(gate) + 1.0) * up).astype(BF16)
            part = _dot(act, wfo_ref[lo:hi, :])
            f = part if f is None else f + part
        return f

    def epilogue(k, f):
        y = (load(k) + _rms(f) * (rows_of(g_ref, k) * gpost_ref[...])).reshape(Rs, Ls, D_MODEL)
        if R > 1:
            o_ref[k * Rs:(k + 1) * Rs] = y
        else:
            o_ref[:, k * Ls:(k + 1) * Ls, :] = y

    hb = prologue(0)
    for k in range(n_sub):
        hb_next = prologue(k + 1) if k + 1 < n_sub else None
        f = matmuls(hb)
        epilogue(k, f)
        hb = hb_next


def _ffn(x, mod, layer, wts, *, R, Lr, n_sub):
    B, S, _ = x.shape

    def mod_spec(j):
        return pl.BlockSpec((None, None, R, 1, D_MODEL), lambda b, s: (layer, j, b, 0, 0))

    return pl.pallas_call(
        functools.partial(_ffn_kernel, R=R, Lr=Lr, n_sub=n_sub),
        grid=(B // R, S // Lr),
        in_specs=[
            pl.BlockSpec((R, Lr, D_MODEL), lambda b, s: (b, s, 0)),
            mod_spec(3), mod_spec(4), mod_spec(5),
            _const_spec((1, D_MODEL), layer),
            _const_spec((1, D_MODEL), layer),
            _const_spec((D_MODEL, 2 * FFN_HIDDEN), layer),
            _const_spec((FFN_HIDDEN, D_MODEL), layer),
        ],
        out_specs=pl.BlockSpec((R, Lr, D_MODEL), lambda b, s: (b, s, 0)),
        out_shape=jax.ShapeDtypeStruct((B, S, D_MODEL), F32),
        compiler_params=pltpu.CompilerParams(
            dimension_semantics=("parallel", "parallel"), vmem_limit_bytes=VMEM_LIMIT),
        name="ffn",
    )(x, mod, mod, mod, wts["g_pre_ffn"], wts["g_post_ffn"], wts["w_ffn_in"], wts["w_ffn_out"])


def _pair_layout(a):
    lead = a.shape[:-1]
    return jnp.swapaxes(a.reshape(lead + (2, 2, SW_HALF)), -3, -2).reshape(lead + (2 * SW_DH,))


def _query_layout(wq):
    lead = wq.shape[:-1]
    w = wq.reshape(lead + (SW_KV, SW_REP, 2, SW_HALF))
    return jnp.moveaxis(w, -4, -2).reshape(lead + (SW_W,))


def _rope_tables(pos):
    inv = ROPE_THETA ** (-jnp.arange(SW_HALF, dtype=F32) / SW_HALF)
    ang = pos.astype(F32)[:, None] * inv[None, :]
    cos, sin = jnp.cos(ang), jnp.sin(ang)
    cos_t = jnp.concatenate([cos, cos, cos, cos], axis=1)
    sin_t = jnp.concatenate([-sin, -sin, sin, sin], axis=1)
    return cos_t, sin_t


def _prep_weights(w_in, b_igate, b_fgate, g_pre_mix, g_post_mix, g_pre_ffn, g_post_ffn, ln_v_g, ln_v_b,
                  w_spatial, b_spatial, swa_sinks, w_br_mlstm, w_br_gmlp, w_br_swa, w_out, w_ffn_in, w_ffn_out):
    offs = np.cumsum((0,) + SPLIT_SIZES)
    seg = [w_in[..., offs[i]:offs[i + 1]] for i in range(len(SPLIT_SIZES))]
    mq, mk, mv, mo, mi, mf, su, sv, wq, wk, wv, gl = seg
    gates = jnp.concatenate([mi, mf], axis=-1)
    win = jnp.concatenate([a.astype(BF16) for a in
                           (mq, mv, 0.5 * mo, su, sv, _query_layout(wq), _pair_layout(wk), wv, 0.5 * gl)], axis=-1)
    w_ffn_in = w_ffn_in * jnp.where(jnp.arange(2 * FFN_HIDDEN) < FFN_HIDDEN, 0.5, 1.0).astype(F32)
    bg = jnp.concatenate([b_igate, b_fgate], axis=-1)
    row = lambda a: a.reshape(DEPTH, 1, a.shape[-1])
    wbs = jnp.swapaxes(w_br_swa.reshape(DEPTH, SW_KV, SW_REP, SW_DH, D_MODEL), 1, 2).reshape(DEPTH, SW_W, D_MODEL)
    return dict(
        win=win,
        wkt=jnp.concatenate([jnp.swapaxes(mk, 1, 2),
                             jnp.pad(jnp.swapaxes(gates, 1, 2), ((0, 0), (0, GATE_ROWS - N_GATE), (0, 0)))],
                            axis=1).astype(BF16),
        bg_row=jnp.pad(bg, ((0, 0), (0, GATE_ROWS - N_GATE))).reshape(DEPTH, GATE_ROWS, 1),
        g_pre_mix=row(g_pre_mix), g_post_mix=row(g_post_mix),
        g_pre_ffn=row(g_pre_ffn), g_post_ffn=row(g_post_ffn),
        ln_v_g=row(ln_v_g), ln_v_b=row(ln_v_b),
        w_spatial=w_spatial,
        b_spatial_b=jnp.broadcast_to(b_spatial[..., None], b_spatial.shape + (SG_GDIM,)),
        swa_sinks=swa_sinks,
        w_br_mlstm=w_br_mlstm.astype(BF16), w_br_gmlp=w_br_gmlp.astype(BF16),
        w_br_swa=wbs.astype(BF16), w_out=w_out.astype(BF16),
        w_ffn_in=w_ffn_in.astype(BF16), w_ffn_out=w_ffn_out.astype(BF16),
    )


MIX_TOKENS_PROMPT = 512
MIX_SUB_PROMPT = 2
MIX_ROWS_SAMPLE = 4
FFN_TOKENS = 1024
FFN_SUB = 2


def kernel(x_prompt, x_sample, c_prompt, c_sample, state_mlstm_C, state_mlstm_n, state_mlstm_m, cache_swa_k, cache_swa_v, w_ada, b_ada, g_pre_mix, g_post_mix, g_pre_ffn, g_post_ffn, w_in, b_igate, b_fgate, ln_v_g, ln_v_b, w_spatial, b_spatial, swa_sinks, w_br_mlstm, w_br_gmlp, w_br_swa, w_out, w_ffn_in, w_ffn_out):
    Bp, Sp, _ = x_prompt.shape
    Bs, Ss, _ = x_sample.shape
    wts = _prep_weights(w_in, b_igate, b_fgate, g_pre_mix, g_post_mix, g_pre_ffn, g_post_ffn, ln_v_g, ln_v_b,
                        w_spatial, b_spatial, swa_sinks, w_br_mlstm, w_br_gmlp, w_br_swa, w_out,
                        w_ffn_in, w_ffn_out)
    mod = _modulation(jnp.concatenate([c_prompt, c_sample], axis=0), w_ada, b_ada)
    mod = mod.reshape(DEPTH, 6, Bp + Bs, 1, D_MODEL)
    mod_p, mod_s = mod[:, :, :Bp], mod[:, :, Bp:]
    cos_p, sin_p = _rope_tables(jnp.arange(Sp))
    cos_s, sin_s = _rope_tables(PAST_LEN + jnp.arange(Ss))

    m_in = jnp.broadcast_to(state_mlstm_m[..., None], state_mlstm_m.shape + (LANES,))
    ck = _pair_layout(cache_swa_k.reshape(DEPTH, Bs, WINDOW, SW_KVW))
    cv = cache_swa_v.reshape(DEPTH, Bs, WINDOW, SW_KVW)
    state = (state_mlstm_C, state_mlstm_n, m_in, ck, cv)

    xp, xs = x_prompt, x_sample
    st_p, st_s, ms_p, ms_s = None, None, [], []
    for l in range(DEPTH):
        xp, m_p, st_p = _mixer(xp, mod_p, l, wts, cos_p, sin_p, None, st_p,
                               R=1, Lr=MIX_TOKENS_PROMPT, Lg=SG_CHUNK, n_sub=MIX_SUB_PROMPT)
        xp = _ffn(xp, mod_p, l, wts, R=1, Lr=FFN_TOKENS, n_sub=FFN_SUB)
        xs, m_s, st_s = _mixer(xs, mod_s, l, wts, cos_s, sin_s, state, st_s,
                               R=MIX_ROWS_SAMPLE, Lr=Ss, Lg=Ss, n_sub=1)
        xs = _ffn(xs, mod_s, l, wts, R=FFN_TOKENS // Ss, Lr=Ss, n_sub=FFN_SUB)
        ms_p.append(m_p)
        ms_s.append(m_s)

    def unpack(st, ms, B, n_new):
        C, n, k, v = st[:4]
        m = jnp.stack(ms)[..., 0]
        k = _pair_layout(k).reshape(DEPTH, B, n_new, SW_KV, SW_DH)
        return C, n, m, k, v.reshape(DEPTH, B, n_new, SW_KV, SW_DH)

    C_p, n_p, m_p, k_p, v_p = unpack(st_p, ms_p, Bp, WINDOW)
    C_s, n_s, m_s, k_s, v_s = unpack(st_s, ms_s, Bs, Ss)
    return (xp, xs, C_p, n_p, m_p, k_p, v_p, C_s, n_s, m_s, k_s, v_s, st_s[4])
```

```python
import functools

import numpy as np
import jax
import jax.numpy as jnp
from jax import lax
from jax.experimental import pallas as pl
from jax.experimental.pallas import tpu as pltpu

F32 = jnp.float32
BF16 = jnp.bfloat16

D_MODEL = 1024
DEPTH = 4
PAST_LEN = 1024
ML_HEADS = 4
ML_DH = 128
ML_W = ML_HEADS * ML_DH
SG_GROUPS = 4
SG_CHUNK = 128
SG_W = 512
SG_GDIM = SG_W // SG_GROUPS
SW_HEADS = 8
SW_KV = 2
SW_REP = SW_HEADS // SW_KV
SW_DH = 64
SW_HALF = SW_DH // 2
SW_W = SW_HEADS * SW_DH
SW_KVW = SW_KV * SW_DH
SW_CHUNK = 64
WINDOW = 128
BAND = WINDOW + SW_CHUNK
ROPE_THETA = 10000.0
FFN_HIDDEN = 2816
EPS = 1e-6
SPLIT_SIZES = (ML_W, ML_W, ML_W, ML_W, ML_HEADS, ML_HEADS, SG_W, SG_W, SW_W, SW_KVW, SW_KVW, 3 * D_MODEL)

LANES = 128
N_GATE = 2 * ML_HEADS
GATE_ROWS = 16

OFF_MQ = 0
OFF_MV = OFF_MQ + ML_W
OFF_MO = OFF_MV + ML_W
OFF_SU = OFF_MO + ML_W
OFF_SV = OFF_SU + SG_W
OFF_WQ = OFF_SV + SG_W
OFF_WK = OFF_WQ + SW_W
OFF_WV = OFF_WK + SW_KVW
OFF_GL = OFF_WV + SW_KVW
D_IN_PAD = OFF_GL + 3 * D_MODEL

FFN_SPLITS = ((0, 1536), (1536, FFN_HIDDEN))

VMEM_LIMIT = 56 * 1024 * 1024

NT_DIMS = (((1,), (1,)), ((), ()))


def _dot(a, b):
    return jnp.dot(a, b, preferred_element_type=F32)


def _dot_nt(a, b):
    return lax.dot_general(a, b, NT_DIMS, preferred_element_type=F32)


def _sigmoid(x):
    return 0.5 * jnp.tanh(0.5 * x) + 0.5


def _rms(x):
    return x * lax.rsqrt(jnp.mean(x * x, axis=-1, keepdims=True) + EPS)


def _rows(m3, R, Lr):
    W = m3.shape[-1]
    if R == 1:
        return m3.reshape(1, W)
    return jnp.broadcast_to(m3, (R, Lr, W)).reshape(R * Lr, W)


def _split3(x):
    h1 = x.astype(BF16)
    r1 = x - h1.astype(F32)
    h2 = r1.astype(BF16)
    h3 = (r1 - h2.astype(F32)).astype(BF16)
    return h1, h2, h3


class _Fillers:
    def __init__(self, total_weight):
        self._thunks, self._vals, self._left = {}, {}, total_weight

    def add(self, name, thunk):
        self._thunks[name] = thunk

    def get(self, name):
        if name not in self._vals:
            self._vals[name] = self._thunks.pop(name)()
        return self._vals[name]

    def pump(self, weight):
        n = -(-len(self._thunks) * weight // max(self._left, 1))
        self._left -= weight
        for name in list(self._thunks)[:n]:
            self.get(name)

    def drain(self):
        for name in list(self._thunks):
            if name in self._thunks:
                self.get(name)


def _mod_kernel(c_ref, w_ref, b_ref, o_ref):
    c = c_ref[...]
    a = (c * _sigmoid(c)).astype(BF16)
    o_ref[...] = _dot(a, w_ref[...].astype(BF16)) + b_ref[...]


def _modulation(c_all, w_ada, b_ada):
    nb = c_all.shape[0]
    return pl.pallas_call(
        _mod_kernel,
        grid=(DEPTH, 6),
        in_specs=[
            pl.BlockSpec((nb, D_MODEL), lambda l, j: (0, 0)),
            pl.BlockSpec((None, D_MODEL, D_MODEL), lambda l, j: (l, 0, j)),
            pl.BlockSpec((None, None, 1, D_MODEL), lambda l, j: (l, j, 0, 0)),
        ],
        out_specs=pl.BlockSpec((None, None, nb, D_MODEL), lambda l, j: (l, j, 0, 0)),
        out_shape=jax.ShapeDtypeStruct((DEPTH, 6, nb, D_MODEL), F32),
        compiler_params=pltpu.CompilerParams(dimension_semantics=("parallel", "parallel")),
        name="adaln_modulation",
    )(c_all, w_ada, b_ada.reshape(DEPTH, 6, 1, D_MODEL))


def _rope(x, cos, sin_signed):
    T, W = x.shape
    outs = []
    for j in range(W // LANES):
        xb = x[:, j * LANES:(j + 1) * LANES]
        outs.append(xb * cos + pltpu.roll(xb, LANES // 2, 1) * sin_signed)
    return outs[0] if len(outs) == 1 else jnp.concatenate(outs, axis=1)


def _mlstm_unit(qb, kt, v, a_row, b_col, caug, m_prev):
    L = qb.shape[0]
    ti = lax.broadcasted_iota(jnp.int32, (L, L), 0)
    si = lax.broadcasted_iota(jnp.int32, (L, L), 1)
    a_tri = jnp.where(ti >= si, a_row, -jnp.inf)
    g = jnp.maximum(jnp.max(a_tri, axis=-1, keepdims=True), m_prev)
    w_intra = jnp.exp(a_tri - g)
    w_inter = jnp.exp(m_prev - g)
    s = _dot(qb, kt.astype(BF16)) * w_intra
    vaug = jnp.concatenate([v, jnp.ones_like(v)], axis=1).astype(BF16)
    qc = _dot(qb, caug.astype(BF16))
    sv = _dot(s.astype(BF16), vaug)
    num = w_inter * qc[:, :ML_DH] + sv[:, :ML_DH]
    den = w_inter * qc[:, ML_DH:] + sv[:, ML_DH:]
    floor = jnp.exp(-(jnp.broadcast_to(b_col, (L, ML_DH)) + g))
    h = num / jnp.maximum(jnp.abs(den), floor)
    g_last = g[L - 1:L, :]
    ws_row = jnp.exp(a_row - g_last)
    upd = _dot((kt * ws_row).astype(BF16), vaug)
    caug_new = jnp.exp(m_prev - g_last) * caug + upd
    return h, caug_new, b_col[L - 1:L, :] + g_last


def _swa_chunk(qstack, kg, vaug, sink_cols, mask):
    o = None
    e_cols = []
    for g in range(SW_KV):
        s = _dot_nt(qstack, kg[g])
        if mask is not None:
            s = jnp.where(mask, s, -jnp.inf)
        mx = jnp.maximum(jnp.max(s, axis=-1, keepdims=True), sink_cols[g])
        p = jnp.exp(s - mx).astype(BF16)
        e_cols.append(jnp.exp(sink_cols[g] - mx))
        og = _dot(p, vaug[g])
        o = og if o is None else o + og
    lane = lax.broadcasted_iota(jnp.int32, (qstack.shape[0], LANES), 1)
    e = jnp.where(lane < SW_DH, e_cols[0], e_cols[1])
    return o[:, :LANES] / (o[:, LANES:] + e)


def _mixer_kernel(*refs, R, Lr, Lg, has_state, n_seq, n_alias, n_sub):
    it = iter(refs)
    x_ref, sh1_ref, sc1_ref, g1_ref = next(it), next(it), next(it), next(it)
    win_ref, wkt_ref, gpre_ref, gpost_ref = next(it), next(it), next(it), next(it)
    bgr_ref = next(it)
    lng_ref, lnb_ref, wsp_ref, bsp_ref = next(it), next(it), next(it), next(it)
    sinks_ref = next(it)
    wbm_ref, wbg_ref, wbs_ref, wout_ref = next(it), next(it), next(it), next(it)
    cos_ref, sin_ref = next(it), next(it)
    if has_state:
        cin_ref, nin_ref, min_ref, ck_ref, cv_ref = next(it), next(it), next(it), next(it), next(it)
    for _ in range(n_alias):
        next(it)
    xo_ref, cout_ref, nout_ref, m_ref, kn_ref, vn_ref = (next(it), next(it), next(it), next(it), next(it),
                                                         next(it))
    if has_state:
        gv_ref = next(it)
    st_ref, yml_ref, ysg_ref, ysw_ref = next(it), next(it), next(it), next(it)
    if not has_state:
        kprev_ref, vprev_ref = next(it), next(it)

    Ls = Lr // n_sub
    T = R * Ls
    seq = pl.program_id(1)

    @pl.when(seq == 0)
    def _init():
        if has_state:
            for r in range(R):
                for hd in range(ML_HEADS):
                    st_ref[r, hd, :, :ML_DH] = cin_ref[r, hd]
                    st_ref[r, hd, :, ML_DH:] = jnp.broadcast_to(nin_ref[r, hd:hd + 1, :], (ML_DH, ML_DH)).T
            m_ref[...] = min_ref[...]
        else:
            st_ref[...] = jnp.zeros_like(st_ref)
            m_ref[...] = jnp.zeros_like(m_ref)
            kprev_ref[...] = jnp.zeros_like(kprev_ref)
            vprev_ref[...] = jnp.zeros_like(vprev_ref)

    n_sw = Ls // SW_CHUNK
    w_ml, w_sg, w_sw = (3, 1, 3) if Ls > SW_CHUNK else (1, 1, 1)
    fill = _Fillers(n_sub * (R * ML_HEADS * w_ml + SG_GROUPS * w_sg + R * n_sw * w_sw))

    def toks(k):
        return slice(k * Ls, (k + 1) * Ls)

    def load_x(k):
        return x_ref[:, toks(k), :].reshape(T, D_MODEL)

    def norm(k):
        scale = gpre_ref[...] * (1.0 + _rows(sc1_ref[...], R, Ls))
        return (_rms(load_x(k)) * scale + _rows(sh1_ref[...], R, Ls)).astype(BF16)

    def queue_head(k):
        hb = lambda: fill.get(("hb", k))
        fill.add(("kg", k), lambda: _dot_nt(wkt_ref[...], hb()))
        fill.add(("grow", k), lambda: fill.get(("kg", k))[ML_W:, :] + bgr_ref[...])
        fill.add(("mq", k), lambda: _dot(hb(), win_ref[:, OFF_MQ:OFF_MQ + ML_W]).astype(BF16))
        fill.add(("mkt", k), lambda: fill.get(("kg", k))[:ML_W, :] * (ML_DH ** -0.5))
        fill.add(("mv", k), lambda: _dot(hb(), win_ref[:, OFF_MV:OFF_MV + ML_W]))

    def queue_body(k, with_gates):
        hb = lambda: fill.get(("hb", k))

        def proj(lo, width):
            return _dot(hb(), win_ref[:, lo:lo + width])

        def gmlp_inputs():
            sv = proj(OFF_SV, SG_W)
            mu = jnp.mean(sv, axis=-1, keepdims=True)
            svc = sv - mu
            var = jnp.mean(svc * svc, axis=-1, keepdims=True)
            return svc * lax.rsqrt(var + EPS) * lng_ref[...] + lnb_ref[...]

        def rope_tables():
            cos, sin = cos_ref[toks(k), :], sin_ref[toks(k), :]
            if R > 1:
                cos = jnp.concatenate([cos] * R, axis=0)
                sin = jnp.concatenate([sin] * R, axis=0)
            return cos, sin

        if not with_gates:
            fill.add(("mo", k), lambda: _sigmoid(proj(OFF_MO, ML_W)))
            fill.add(("su", k), lambda: proj(OFF_SU, SG_W))
            fill.add(("vnorm", k), gmlp_inputs)
            fill.add(("wq", k), lambda: (_rope(proj(OFF_WQ, SW_W), *rope_tables()) * (SW_DH ** -0.5)).astype(BF16))
            fill.add(("wk", k), lambda: _rope(proj(OFF_WK, SW_KVW), *rope_tables()))
            fill.add(("wv", k), lambda: proj(OFF_WV, SW_KVW))
        else:
            for i in range(3):
                fill.add(("gl", k, i), functools.partial(
                    lambda i: _sigmoid(proj(OFF_GL + i * D_MODEL, D_MODEL)), i))

    def queue_tail(k):
        rows = slice(k * T, (k + 1) * T)
        gate = lambda b: fill.get(("gl", k, b))
        fill.add(("sw", k), lambda: _dot(ysw_ref[rows, :], wbs_ref[...]))

        def out_proj():
            merged = gate(0) * fill.get(("ml", k))
            merged = merged + gate(1) * fill.get(("sg", k))
            merged = merged + gate(2) * fill.get(("sw", k))
            return _dot(merged.astype(BF16), wout_ref[...])

        def emit():
            y = load_x(k) + _rms(fill.get(("o", k))) * (_rows(g1_ref[...], R, Ls) * gpost_ref[...])
            xo_ref[:, toks(k), :] = y.reshape(R, Ls, D_MODEL)
            return None

        fill.add(("o", k), out_proj)
        fill.add(("emit", k), emit)

    ri = lax.broadcasted_iota(jnp.int32, (Ls, Ls), 0)
    ci = lax.broadcasted_iota(jnp.int32, (Ls, Ls), 1)
    triu = jnp.where(ri <= ci, 1.0, 0.0).astype(BF16)
    triu3 = jnp.concatenate([triu] * 3, axis=0)
    gi = lax.broadcasted_iota(jnp.int32, (Lg, Lg), 0)
    gj = lax.broadcasted_iota(jnp.int32, (Lg, Lg), 1)
    n_q = SW_REP * SW_CHUNK
    rid = lax.broadcasted_iota(jnp.int32, (n_q, 1), 0) // SW_CHUNK
    sink_cols = []
    for g in range(SW_KV):
        col = jnp.full((n_q, 1), sinks_ref[SW_REP * g + SW_REP - 1], F32)
        for i in range(SW_REP - 2, -1, -1):
            col = jnp.where(rid == i, sinks_ref[SW_REP * g + i], col)
        sink_cols.append(col)
    kpos_chunk = lax.broadcasted_iota(jnp.int32, (n_q, BAND), 1) // SW_CHUNK

    def mixers(k):
        rows = slice(k * T, (k + 1) * T)
        grow = fill.get(("grow", k))
        lf_row = jax.nn.log_sigmoid(grow)
        mqb, mkt, mv = fill.get(("mq", k)), fill.get(("mkt", k)), fill.get(("mv", k))
        for r in range(R):
            rs = slice(r * Ls, (r + 1) * Ls)
            brow_all = _dot(jnp.concatenate(_split3(lf_row[:, rs]), axis=1), triu3)
            bcol_all = brow_all.T
            for hd in range(ML_HEADS):
                hs = slice(hd * ML_DH, (hd + 1) * ML_DH)
                fg = ML_HEADS + hd
                fill.pump(w_ml)
                h_ml, caug_new, m_new = _mlstm_unit(
                    mqb[rs, hs], mkt[hs, rs], mv[rs, hs],
                    grow[hd:hd + 1, rs] - brow_all[fg:fg + 1, :], bcol_all[:, fg:fg + 1],
                    st_ref[r, hd], m_ref[r, hd:hd + 1, 0:1])
                st_ref[r, hd] = caug_new
                m_ref[r, hd:hd + 1, :] = jnp.broadcast_to(m_new, (1, LANES))
                yml_ref[k * T + r * Ls:k * T + (r + 1) * Ls, hs] = (
                    fill.get(("mo", k))[rs, hs] * h_ml).astype(BF16)
        fill.add(("ml", k), lambda: _dot(yml_ref[rows, :], wbm_ref[...]))

        vnorm = fill.get(("vnorm", k))
        su = fill.get(("su", k))
        if has_state:
            gv_ref[...] = vnorm.reshape(R, Ls, SG_W)
        n_chunk = T // Lg
        for g in range(SG_GROUPS):
            cs = slice(g * SG_GDIM, (g + 1) * SG_GDIM)
            fill.pump(w_sg)
            w = jnp.where(gi >= gj, wsp_ref[g, :Lg, :Lg], 0.0).astype(BF16)
            vcat = jnp.concatenate([vnorm[c * Lg:(c + 1) * Lg, cs] for c in range(n_chunk)], axis=1)
            z = _dot(w, vcat.astype(BF16))
            for c in range(n_chunk):
                ts = slice(c * Lg, (c + 1) * Lg)
                zc = z[:, c * SG_GDIM:(c + 1) * SG_GDIM] + bsp_ref[g, :Lg, :]
                ysg_ref[k * T + c * Lg:k * T + (c + 1) * Lg, cs] = (su[ts, cs] * zc).astype(BF16)
        fill.add(("sg", k), lambda: _dot(ysg_ref[rows, :], wbg_ref[...]))

        wqb, wk, wv = fill.get(("wq", k)), fill.get(("wk", k)), fill.get(("wv", k))
        for r in range(R):
            if has_state:
                kband = jnp.concatenate([ck_ref[r], wk[r * Ls:(r + 1) * Ls]], axis=0)
                vband = jnp.concatenate([cv_ref[r], wv[r * Ls:(r + 1) * Ls]], axis=0)
            else:
                kband = jnp.concatenate([kprev_ref[...], wk], axis=0)
                vband = jnp.concatenate([vprev_ref[...], wv], axis=0)
            lane = lax.broadcasted_iota(jnp.int32, kband.shape, 1)
            k_group = (lane // SW_HALF) % SW_KV
            v_group = lane // SW_DH
            kg = [jnp.where(k_group == g, kband, 0.0).astype(BF16) for g in range(SW_KV)]
            vaug = [jnp.concatenate([jnp.where(v_group == g, vband, 0.0),
                                     jnp.where(v_group == g, 1.0, 0.0)], axis=1).astype(BF16)
                    for g in range(SW_KV)]
            for j in range(n_sw):
                t0 = r * Ls + j * SW_CHUNK
                fill.pump(w_sw)
                qstack = jnp.concatenate([wqb[t0:t0 + SW_CHUNK, i * LANES:(i + 1) * LANES]
                                          for i in range(SW_REP)], axis=0)
                needs_mask = not has_state and k == 0 and j < WINDOW // SW_CHUNK
                mask = kpos_chunk >= (WINDOW // SW_CHUNK) - (seq * n_sub * n_sw + j) if needs_mask else None
                ks = slice(j * SW_CHUNK, j * SW_CHUNK + BAND)
                y = _swa_chunk(qstack, [a[ks] for a in kg], [a[ks] for a in vaug], sink_cols, mask)
                for i in range(SW_REP):
                    ysw_ref[k * T + t0:k * T + t0 + SW_CHUNK, i * LANES:(i + 1) * LANES] = (
                        y[i * SW_CHUNK:(i + 1) * SW_CHUNK].astype(BF16))
        if has_state:
            kn_ref[...] = wk.reshape(R, Ls, SW_KVW)
            vn_ref[...] = wv.reshape(R, Ls, SW_KVW)
        else:
            kprev_ref[...] = wk[T - WINDOW:]
            vprev_ref[...] = wv[T - WINDOW:]

    fill.add(("hb", 0), lambda: norm(0))
    queue_head(0)
    for k in range(n_sub):
        for name in ("grow", "mq", "mkt", "mv"):
            fill.get((name, k))
        queue_body(k, with_gates=False)
        if k + 1 < n_sub:
            fill.add(("hb", k + 1), functools.partial(norm, k + 1))
            fill.get(("hb", k + 1))
            queue_head(k + 1)
        queue_body(k, with_gates=True)
        mixers(k)
        queue_tail(k)
    fill.drain()

    @pl.when(seq == n_seq - 1)
    def _emit_state():
        if not has_state:
            kn_ref[0] = kprev_ref[...]
            vn_ref[0] = vprev_ref[...]
        for r in range(R):
            for hd in range(ML_HEADS):
                cout_ref[r, hd] = st_ref[r, hd, :, :ML_DH]
                nout_ref[r, hd:hd + 1, :] = st_ref[r, hd, :, ML_DH:].T[0:1, :]


def _const_spec(shape, layer):
    nd = len(shape)
    return pl.BlockSpec((None,) + tuple(shape), lambda b, s: (layer,) + (0,) * nd,
                        pipeline_mode=pl.Buffered(1))


def _mixer(x, mod, layer, wts, cos_t, sin_t, state, prev, *, R, Lr, Lg, n_sub):
    B, S, _ = x.shape
    n_seq = S // Lr
    has_state = state is not None
    T = R * Lr
    assert n_sub == 1 or R == 1

    def mod_spec(j):
        return pl.BlockSpec((None, None, R, 1, D_MODEL), lambda b, s: (layer, j, b, 0, 0))

    in_specs = [
        pl.BlockSpec((R, Lr, D_MODEL), lambda b, s: (b, s, 0)),
        mod_spec(0), mod_spec(1), mod_spec(2),
        _const_spec((D_MODEL, D_IN_PAD), layer),
        _const_spec((ML_W + GATE_ROWS, D_MODEL), layer),
        _const_spec((1, D_MODEL), layer),
        _const_spec((1, D_MODEL), layer),
        _const_spec((GATE_ROWS, 1), layer),
        _const_spec((1, SG_W), layer),
        _const_spec((1, SG_W), layer),
        _const_spec((SG_GROUPS, SG_CHUNK, SG_CHUNK), layer),
        _const_spec((SG_GROUPS, SG_CHUNK, SG_GDIM), layer),
        pl.BlockSpec(memory_space=pltpu.SMEM),
        _const_spec((ML_W, D_MODEL), layer),
        _const_spec((SG_W, D_MODEL), layer),
        _const_spec((SW_W, D_MODEL), layer),
        _const_spec((D_MODEL, D_MODEL), layer),
        pl.BlockSpec((Lr, LANES), (lambda b, s: (0, 0)) if has_state else (lambda b, s: (s, 0))),
        pl.BlockSpec((Lr, LANES), (lambda b, s: (0, 0)) if has_state else (lambda b, s: (s, 0))),
    ]
    args = [x, mod, mod, mod, wts["win"], wts["wkt"], wts["g_pre_mix"], wts["g_post_mix"],
            wts["bg_row"], wts["ln_v_g"], wts["ln_v_b"], wts["w_spatial"], wts["b_spatial_b"],
            wts["swa_sinks"][layer], wts["w_br_mlstm"], wts["w_br_gmlp"], wts["w_br_swa"], wts["w_out"],
            cos_t, sin_t]
    n_new = Lr if has_state else WINDOW
    stacked = [
        ((DEPTH, B, ML_HEADS, ML_DH, ML_DH), (None, R, ML_HEADS, ML_DH, ML_DH), lambda b, s: (layer, b, 0, 0, 0)),
        ((DEPTH, B, ML_HEADS, ML_DH), (None, R, ML_HEADS, ML_DH), lambda b, s: (layer, b, 0, 0)),
        ((DEPTH, B, n_new, SW_KVW), (None, R, n_new, SW_KVW), lambda b, s: (layer, b, 0, 0)),
        ((DEPTH, B, n_new, SW_KVW), (None, R, n_new, SW_KVW), lambda b, s: (layer, b, 0, 0)),
    ]
    if has_state:
        stacked.append(((DEPTH, B, S, SG_W), (None, R, Lr, SG_W), lambda b, s: (layer, b, s, 0)))
    st_shapes = [jax.ShapeDtypeStruct(shp, F32) for shp, _, _ in stacked]
    st_specs = [pl.BlockSpec(blk, imap) for _, blk, imap in stacked]
    x_out = (jax.ShapeDtypeStruct((B, S, D_MODEL), F32), pl.BlockSpec((R, Lr, D_MODEL), lambda b, s: (b, s, 0)))
    m_out = (jax.ShapeDtypeStruct((B, ML_HEADS, LANES), F32),
             pl.BlockSpec((R, ML_HEADS, LANES), lambda b, s: (b, 0, 0)))
    out_shape = [x_out[0], st_shapes[0], st_shapes[1], m_out[0]] + st_shapes[2:]
    out_specs = [x_out[1], st_specs[0], st_specs[1], m_out[1]] + st_specs[2:]
    stacked_out_idx = [1, 2] + list(range(4, 4 + len(stacked) - 2))
    scratch = [pltpu.VMEM((R, ML_HEADS, ML_DH, 2 * ML_DH), F32),
               pltpu.VMEM((T, ML_W), BF16), pltpu.VMEM((T, SG_W), BF16), pltpu.VMEM((T, SW_W), BF16)]
    if has_state:
        c_in, n_in, m_in, cache_k, cache_v = state
        in_specs += [
            pl.BlockSpec((None, R, ML_HEADS, ML_DH, ML_DH), lambda b, s: (layer, b, 0, 0, 0)),
            pl.BlockSpec((None, R, ML_HEADS, ML_DH), lambda b, s: (layer, b, 0, 0)),
            pl.BlockSpec((None, R, ML_HEADS, LANES), lambda b, s: (layer, b, 0, 0)),
            pl.BlockSpec((None, R, WINDOW, SW_KVW), lambda b, s: (layer, b, 0, 0)),
            pl.BlockSpec((None, R, WINDOW, SW_KVW), lambda b, s: (layer, b, 0, 0)),
        ]
        args += [c_in, n_in, m_in, cache_k, cache_v]
    else:
        scratch += [pltpu.VMEM((WINDOW, SW_KVW), F32), pltpu.VMEM((WINDOW, SW_KVW), F32)]
    if prev is None:
        prev = [jnp.zeros(sds.shape, sds.dtype) for sds in st_shapes]
    aliases = {}
    for arr, oi in zip(prev, stacked_out_idx):
        aliases[len(args)] = oi
        in_specs.append(pl.BlockSpec(memory_space=pl.ANY))
        args.append(arr)

    outs = pl.pallas_call(
        functools.partial(_mixer_kernel, R=R, Lr=Lr, Lg=Lg, has_state=has_state, n_seq=n_seq,
                          n_alias=len(aliases), n_sub=n_sub),
        grid=(B // R, n_seq),
        in_specs=in_specs,
        out_specs=out_specs,
        out_shape=out_shape,
        scratch_shapes=scratch,
        input_output_aliases=aliases,
        compiler_params=pltpu.CompilerParams(
            dimension_semantics=("parallel", "arbitrary"), vmem_limit_bytes=VMEM_LIMIT),
        name="mixer_sample" if has_state else "mixer_prompt",
    )(*args)
    return outs[0], outs[3], [outs[i] for i in stacked_out_idx]


def _ffn_kernel(x_ref, sh_ref, sc_ref, g_ref, gpre_ref, gpost_ref, wfi_ref, wfo_ref, o_ref, *, R, Lr, n_sub):
    Rs = R // n_sub if R > 1 else 1
    Ls = Lr if R > 1 else Lr // n_sub
    Ts = Rs * Ls

    def rows_of(ref, k):
        return _rows(ref[k * Rs:(k + 1) * Rs] if R > 1 else ref[...], Rs, Ls)

    def load(k):
        xk = x_ref[k * Rs:(k + 1) * Rs] if R > 1 else x_ref[:, k * Ls:(k + 1) * Ls, :]
        return xk.reshape(Ts, D_MODEL)

    def prologue(k):
        x = load(k)
        scale = gpre_ref[...] * (1.0 + rows_of(sc_ref, k))
        return (_rms(x) * scale + rows_of(sh_ref, k)).astype(BF16)

    def matmuls(hb):
        f = None
        for lo, hi in FFN_SPLITS:
            gate = _dot(hb, wfi_ref[:, lo:hi])
            up = _dot(hb, wfi_ref[:, FFN_HIDDEN + lo:FFN_HIDDEN + hi])
            act = (gate * (jnp.tanh(gate) + 1.0) * up).astype(BF16)
            part = _dot(act, wfo_ref[lo:hi, :])
            f = part if f is None else f + part
        return f

    def epilogue(k, f):
        y = (load(k) + _rms(f) * (rows_of(g_ref, k) * gpost_ref[...])).reshape(Rs, Ls, D_MODEL)
        if R > 1:
            o_ref[k * Rs:(k + 1) * Rs] = y
        else:
            o_ref[:, k * Ls:(k + 1) * Ls, :] = y

    hb = prologue(0)
    for k in range(n_sub):
        hb_next = prologue(k + 1) if k + 1 < n_sub else None
        f = matmuls(hb)
        epilogue(k, f)
        hb = hb_next


def _ffn(x, mod, layer, wts, *, R, Lr, n_sub):
    B, S, _ = x.shape

    def mod_spec(j):
        return pl.BlockSpec((None, None, R, 1, D_MODEL), lambda b, s: (layer, j, b, 0, 0))

    return pl.pallas_call(
        functools.partial(_ffn_kernel, R=R, Lr=Lr, n_sub=n_sub),
        grid=(B // R, S // Lr),
        in_specs=[
            pl.BlockSpec((R, Lr, D_MODEL), lambda b, s: (b, s, 0)),
            mod_spec(3), mod_spec(4), mod_spec(5),
            _const_spec((1, D_MODEL), layer),
            _const_spec((1, D_MODEL), layer),
            _const_spec((D_MODEL, 2 * FFN_HIDDEN), layer),
            _const_spec((FFN_HIDDEN, D_MODEL), layer),
        ],
        out_specs=pl.BlockSpec((R, Lr, D_MODEL), lambda b, s: (b, s, 0)),
        out_shape=jax.ShapeDtypeStruct((B, S, D_MODEL), F32),
        compiler_params=pltpu.CompilerParams(
            dimension_semantics=("parallel", "parallel"), vmem_limit_bytes=VMEM_LIMIT),
        name="ffn",
    )(x, mod, mod, mod, wts["g_pre_ffn"], wts["g_post_ffn"], wts["w_ffn_in"], wts["w_ffn_out"])


def _pair_layout(a):
    lead = a.shape[:-1]
    return jnp.swapaxes(a.reshape(lead + (2, 2, SW_HALF)), -3, -2).reshape(lead + (2 * SW_DH,))


def _query_layout(wq):
    lead = wq.shape[:-1]
    w = wq.reshape(lead + (SW_KV, SW_REP, 2, SW_HALF))
    return jnp.moveaxis(w, -4, -2).reshape(lead + (SW_W,))


def _rope_tables(pos):
    inv = ROPE_THETA ** (-jnp.arange(SW_HALF, dtype=F32) / SW_HALF)
    ang = pos.astype(F32)[:, None] * inv[None, :]
    cos, sin = jnp.cos(ang), jnp.sin(ang)
    cos_t = jnp.concatenate([cos, cos, cos, cos], axis=1)
    sin_t = jnp.concatenate([-sin, -sin, sin, sin], axis=1)
    return cos_t, sin_t


def _prep_weights(w_in, b_igate, b_fgate, g_pre_mix, g_post_mix, g_pre_ffn, g_post_ffn, ln_v_g, ln_v_b,
                  w_spatial, b_spatial, swa_sinks, w_br_mlstm, w_br_gmlp, w_br_swa, w_out, w_ffn_in, w_ffn_out):
    offs = np.cumsum((0,) + SPLIT_SIZES)
    seg = [w_in[..., offs[i]:offs[i + 1]] for i in range(len(SPLIT_SIZES))]
    mq, mk, mv, mo, mi, mf, su, sv, wq, wk, wv, gl = seg
    gates = jnp.concatenate([mi, mf], axis=-1)
    win = jnp.concatenate([a.astype(BF16) for a in
                           (mq, mv, mo, su, sv, _query_layout(wq), _pair_layout(wk), wv, gl)], axis=-1)
    w_ffn_in = w_ffn_in * jnp.where(jnp.arange(2 * FFN_HIDDEN) < FFN_HIDDEN, 0.5, 1.0).astype(F32)
    bg = jnp.concatenate([b_igate, b_fgate], axis=-1)
    row = lambda a: a.reshape(DEPTH, 1, a.shape[-1])
    wbs = jnp.swapaxes(w_br_swa.reshape(DEPTH, SW_KV, SW_REP, SW_DH, D_MODEL), 1, 2).reshape(DEPTH, SW_W, D_MODEL)
    return dict(
        win=win,
        wkt=jnp.concatenate([jnp.swapaxes(mk, 1, 2),
                             jnp.pad(jnp.swapaxes(gates, 1, 2), ((0, 0), (0, GATE_ROWS - N_GATE), (0, 0)))],
                            axis=1).astype(BF16),
        bg_row=jnp.pad(bg, ((0, 0), (0, GATE_ROWS - N_GATE))).reshape(DEPTH, GATE_ROWS, 1),
        g_pre_mix=row(g_pre_mix), g_post_mix=row(g_post_mix),
        g_pre_ffn=row(g_pre_ffn), g_post_ffn=row(g_post_ffn),
        ln_v_g=row(ln_v_g), ln_v_b=row(ln_v_b),
        w_spatial=w_spatial,
        b_spatial_b=jnp.broadcast_to(b_spatial[..., None], b_spatial.shape + (SG_GDIM,)),
        swa_sinks=swa_sinks,
        w_br_mlstm=w_br_mlstm.astype(BF16), w_br_gmlp=w_br_gmlp.astype(BF16),
        w_br_swa=wbs.astype(BF16), w_out=w_out.astype(BF16),
        w_ffn_in=w_ffn_in.astype(BF16), w_ffn_out=w_ffn_out.astype(BF16),
    )


MIX_TOKENS_PROMPT = 512
MIX_SUB_PROMPT = 2
MIX_ROWS_SAMPLE = 4
FFN_TOKENS = 1024
FFN_SUB = 2


def kernel(x_prompt, x_sample, c_prompt, c_sample, state_mlstm_C, state_mlstm_n, state_mlstm_m, cache_swa_k, cache_swa_v, w_ada, b_ada, g_pre_mix, g_post_mix, g_pre_ffn, g_post_ffn, w_in, b_igate, b_fgate, ln_v_g, ln_v_b, w_spatial, b_spatial, swa_sinks, w_br_mlstm, w_br_gmlp, w_br_swa, w_out, w_ffn_in, w_ffn_out):
    Bp, Sp, _ = x_prompt.shape
    Bs, Ss, _ = x_sample.shape
    wts = _prep_weights(w_in, b_igate, b_fgate, g_pre_mix, g_post_mix, g_pre_ffn, g_post_ffn, ln_v_g, ln_v_b,
                        w_spatial, b_spatial, swa_sinks, w_br_mlstm, w_br_gmlp, w_br_swa, w_out,
                        w_ffn_in, w_ffn_out)
    mod = _modulation(jnp.concatenate([c_prompt, c_sample], axis=0), w_ada, b_ada)
    mod = mod.reshape(DEPTH, 6, Bp + Bs, 1, D_MODEL)
    mod_p, mod_s = mod[:, :, :Bp], mod[:, :, Bp:]
    cos_p, sin_p = _rope_tables(jnp.arange(Sp))
    cos_s, sin_s = _rope_tables(PAST_LEN + jnp.arange(Ss))

    m_in = jnp.broadcast_to(state_mlstm_m[..., None], state_mlstm_m.shape + (LANES,))
    ck = _pair_layout(cache_swa_k.reshape(DEPTH, Bs, WINDOW, SW_KVW))
    cv = cache_swa_v.reshape(DEPTH, Bs, WINDOW, SW_KVW)
    state = (state_mlstm_C, state_mlstm_n, m_in, ck, cv)

    xp, xs = x_prompt, x_sample
    st_p, st_s, ms_p, ms_s = None, None, [], []
    for l in range(DEPTH):
        xp, m_p, st_p = _mixer(xp, mod_p, l, wts, cos_p, sin_p, None, st_p,
                               R=1, Lr=MIX_TOKENS_PROMPT, Lg=SG_CHUNK, n_sub=MIX_SUB_PROMPT)
        xp = _ffn(xp, mod_p, l, wts, R=1, Lr=FFN_TOKENS, n_sub=FFN_SUB)
        xs, m_s, st_s = _mixer(xs, mod_s, l, wts, cos_s, sin_s, state, st_s,
                               R=MIX_ROWS_SAMPLE, Lr=Ss, Lg=Ss, n_sub=1)
        xs = _ffn(xs, mod_s, l, wts, R=FFN_TOKENS // Ss, Lr=Ss, n_sub=FFN_SUB)
        ms_p.append(m_p)
        ms_s.append(m_s)

    def unpack(st, ms, B, n_new):
        C, n, k, v = st[:4]
        m = jnp.stack(ms)[..., 0]
        k = _pair_layout(k).reshape(DEPTH, B, n_new, SW_KV, SW_DH)
        return C, n, m, k, v.reshape(DEPTH, B, n_new, SW_KV, SW_DH)

    C_p, n_p, m_p, k_p, v_p = unpack(st_p, ms_p, Bp, WINDOW)
    C_s, n_s, m_s, k_s, v_s = unpack(st_s, ms_s, Bs, Ss)
    return (xp, xs, C_p, n_p, m_p, k_p, v_p, C_s, n_s, m_s, k_s, v_s, st_s[4])
```

```python
import functools

import numpy as np
import jax
import jax.numpy as jnp
from jax import lax
from jax.experimental import pallas as pl
from jax.experimental.pallas import tpu as pltpu

F32 = jnp.float32
BF16 = jnp.bfloat16

D_MODEL = 1024
DEPTH = 4
PAST_LEN = 1024
ML_HEADS = 4
ML_DH = 128
ML_W = ML_HEADS * ML_DH
SG_GROUPS = 4
SG_CHUNK = 128
SG_W = 512
SG_GDIM = SG_W // SG_GROUPS
SW_HEADS = 8
SW_KV = 2
SW_REP = SW_HEADS // SW_KV
SW_DH = 64
SW_HALF = SW_DH // 2
SW_W = SW_HEADS * SW_DH
SW_KVW = SW_KV * SW_DH
SW_CHUNK = 64
WINDOW = 128
BAND = WINDOW + SW_CHUNK
ROPE_THETA = 10000.0
FFN_HIDDEN = 2816
EPS = 1e-6
SPLIT_SIZES = (ML_W, ML_W, ML_W, ML_W, ML_HEADS, ML_HEADS, SG_W, SG_W, SW_W, SW_KVW, SW_KVW, 3 * D_MODEL)

LANES = 128
N_GATE = 2 * ML_HEADS
GATE_ROWS = 16

OFF_MQ = 0
OFF_MV = OFF_MQ + ML_W
OFF_MO = OFF_MV + ML_W
OFF_SU = OFF_MO + ML_W
OFF_SV = OFF_SU + SG_W
OFF_WQ = OFF_SV + SG_W
OFF_WK = OFF_WQ + SW_W
OFF_WV = OFF_WK + SW_KVW
OFF_GL = OFF_WV + SW_KVW
WIN_PIECES = ((OFF_MQ, ML_W), (OFF_MV, ML_W), (OFF_MO, ML_W), (OFF_SU, SG_W), (OFF_SV, SG_W), (OFF_WQ, SW_W),
              (OFF_WK, SW_KVW), (OFF_WV, SW_KVW), (OFF_GL, 3 * D_MODEL))

FFN_SPLITS = ((0, 1536), (1536, FFN_HIDDEN))

VMEM_LIMIT = 56 * 1024 * 1024

NT_DIMS = (((1,), (1,)), ((), ()))


def _dot(a, b):
    return jnp.dot(a, b, preferred_element_type=F32)


def _dot_nt(a, b):
    return lax.dot_general(a, b, NT_DIMS, preferred_element_type=F32)


def _sigmoid(x):
    return 0.5 * jnp.tanh(0.5 * x) + 0.5


def _rms(x):
    return x * lax.rsqrt(jnp.mean(x * x, axis=-1, keepdims=True) + EPS)


def _rows(m3, R, Lr):
    W = m3.shape[-1]
    if R == 1:
        return m3.reshape(1, W)
    return jnp.broadcast_to(m3, (R, Lr, W)).reshape(R * Lr, W)


def _split3(x):
    h1 = x.astype(BF16)
    r1 = x - h1.astype(F32)
    h2 = r1.astype(BF16)
    h3 = (r1 - h2.astype(F32)).astype(BF16)
    return h1, h2, h3


class _Fillers:
    def __init__(self, total_weight):
        self._thunks, self._vals, self._left = {}, {}, total_weight

    def add(self, name, thunk):
        self._thunks[name] = thunk

    def get(self, name):
        if name not in self._vals:
            self._vals[name] = self._thunks.pop(name)()
        return self._vals[name]

    def pump(self, weight):
        n = -(-len(self._thunks) * weight // max(self._left, 1))
        self._left -= weight
        for name in list(self._thunks)[:n]:
            self.get(name)

    def drain(self):
        for name in list(self._thunks):
            if name in self._thunks:
                self.get(name)


def _mod_kernel(c_ref, w_ref, b_ref, o_ref):
    c = c_ref[...]
    a = (c * _sigmoid(c)).astype(BF16)
    o_ref[...] = _dot(a, w_ref[...].astype(BF16)) + b_ref[...]


def _modulation(c_all, w_ada, b_ada):
    nb = c_all.shape[0]
    return pl.pallas_call(
        _mod_kernel,
        grid=(DEPTH, 6),
        in_specs=[
            pl.BlockSpec((nb, D_MODEL), lambda l, j: (0, 0)),
            pl.BlockSpec((None, D_MODEL, D_MODEL), lambda l, j: (l, 0, j)),
            pl.BlockSpec((None, None, 1, D_MODEL), lambda l, j: (l, j, 0, 0)),
        ],
        out_specs=pl.BlockSpec((None, None, nb, D_MODEL), lambda l, j: (l, j, 0, 0)),
        out_shape=jax.ShapeDtypeStruct((DEPTH, 6, nb, D_MODEL), F32),
        compiler_params=pltpu.CompilerParams(dimension_semantics=("parallel", "parallel")),
        name="adaln_modulation",
    )(c_all, w_ada, b_ada.reshape(DEPTH, 6, 1, D_MODEL))


def _rope(x, cos, sin_signed):
    T, W = x.shape
    outs = []
    for j in range(W // LANES):
        xb = x[:, j * LANES:(j + 1) * LANES]
        outs.append(xb * cos + pltpu.roll(xb, LANES // 2, 1) * sin_signed)
    return outs[0] if len(outs) == 1 else jnp.concatenate(outs, axis=1)


def _mlstm_unit(qb, kt, v, a_row, b_col, caug, m_prev):
    L = qb.shape[0]
    ti = lax.broadcasted_iota(jnp.int32, (L, L), 0)
    si = lax.broadcasted_iota(jnp.int32, (L, L), 1)
    a_tri = jnp.where(ti >= si, a_row, -jnp.inf)
    g = jnp.maximum(jnp.max(a_tri, axis=-1, keepdims=True), m_prev)
    w_intra = jnp.exp(a_tri - g)
    w_inter = jnp.exp(m_prev - g)
    s = _dot(qb, kt.astype(BF16)) * w_intra
    vaug = jnp.concatenate([v, jnp.ones_like(v)], axis=1).astype(BF16)
    qc = _dot(qb, caug.astype(BF16))
    sv = _dot(s.astype(BF16), vaug)
    num = w_inter * qc[:, :ML_DH] + sv[:, :ML_DH]
    den = w_inter * qc[:, ML_DH:] + sv[:, ML_DH:]
    floor = jnp.exp(-(jnp.broadcast_to(b_col, (L, ML_DH)) + g))
    h = num / jnp.maximum(jnp.abs(den), floor)
    g_last = g[L - 1:L, :]
    ws_row = jnp.exp(a_row - g_last)
    upd = _dot((kt * ws_row).astype(BF16), vaug)
    caug_new = jnp.exp(m_prev - g_last) * caug + upd
    return h, caug_new, b_col[L - 1:L, :] + g_last


def _swa_chunk(qstack, kg, vaug, sink_cols, mask):
    o = None
    e_cols = []
    for g in range(SW_KV):
        s = _dot_nt(qstack, kg[g])
        if mask is not None:
            s = jnp.where(mask, s, -jnp.inf)
        mx = jnp.maximum(jnp.max(s, axis=-1, keepdims=True), sink_cols[g])
        p = jnp.exp(s - mx).astype(BF16)
        e_cols.append(jnp.exp(sink_cols[g] - mx))
        og = _dot(p, vaug[g])
        o = og if o is None else o + og
    lane = lax.broadcasted_iota(jnp.int32, (qstack.shape[0], LANES), 1)
    e = jnp.where(lane < SW_DH, e_cols[0], e_cols[1])
    return o[:, :LANES] / (o[:, LANES:] + e)


def _mixer_kernel(*refs, R, Lr, Lg, has_state, n_seq, n_alias, n_sub):
    it = iter(refs)
    x_ref, sh1_ref, sc1_ref, g1_ref = next(it), next(it), next(it), next(it)
    win_refs = [next(it) for _ in WIN_PIECES]
    wkt_ref, gpre_ref, gpost_ref = next(it), next(it), next(it)

    def wcols(lo, width):
        for (off, w), ref in zip(WIN_PIECES, win_refs):
            if off <= lo and lo + width <= off + w:
                return ref[:, lo - off:lo - off + width]
        raise ValueError((lo, width))
    bgr_ref = next(it)
    lng_ref, lnb_ref, wsp_ref, bsp_ref = next(it), next(it), next(it), next(it)
    sinks_ref = next(it)
    wbm_ref, wbg_ref, wbs_ref, wout_ref = next(it), next(it), next(it), next(it)
    cos_ref, sin_ref = next(it), next(it)
    if has_state:
        cin_ref, nin_ref, min_ref, ck_ref, cv_ref = next(it), next(it), next(it), next(it), next(it)
    for _ in range(n_alias):
        next(it)
    xo_ref, cout_ref, nout_ref, m_ref, kn_ref, vn_ref = (next(it), next(it), next(it), next(it), next(it),
                                                         next(it))
    if has_state:
        gv_ref = next(it)
    st_ref, yml_ref, ysg_ref, ysw_ref = next(it), next(it), next(it), next(it)
    if not has_state:
        kprev_ref, vprev_ref = next(it), next(it)

    Ls = Lr // n_sub
    T = R * Ls
    seq = pl.program_id(1)

    @pl.when(seq == 0)
    def _init():
        if has_state:
            for r in range(R):
                for hd in range(ML_HEADS):
                    st_ref[r, hd, :, :ML_DH] = cin_ref[r, hd]
                    st_ref[r, hd, :, ML_DH:] = jnp.broadcast_to(nin_ref[r, hd:hd + 1, :], (ML_DH, ML_DH)).T
            m_ref[...] = min_ref[...]
        else:
            st_ref[...] = jnp.zeros_like(st_ref)
            m_ref[...] = jnp.zeros_like(m_ref)
            kprev_ref[...] = jnp.zeros_like(kprev_ref)
            vprev_ref[...] = jnp.zeros_like(vprev_ref)

    n_sw = Ls // SW_CHUNK
    w_ml, w_sg, w_sw = (3, 1, 3) if Ls > SW_CHUNK else (1, 1, 1)
    fill = _Fillers(n_sub * (R * ML_HEADS * w_ml + SG_GROUPS * w_sg + R * n_sw * w_sw))

    def toks(k):
        return slice(k * Ls, (k + 1) * Ls)

    def load_x(k):
        return x_ref[:, toks(k), :].reshape(T, D_MODEL)

    def norm(k):
        scale = gpre_ref[...] * (1.0 + _rows(sc1_ref[...], R, Ls))
        return (_rms(load_x(k)) * scale + _rows(sh1_ref[...], R, Ls)).astype(BF16)

    def queue_head(k):
        hb = lambda: fill.get(("hb", k))
        fill.add(("kg", k), lambda: _dot_nt(wkt_ref[...], hb()))
        fill.add(("grow", k), lambda: fill.get(("kg", k))[ML_W:, :] + bgr_ref[...])
        fill.add(("mq", k), lambda: _dot(hb(), wcols(OFF_MQ, ML_W)).astype(BF16))
        fill.add(("mkt", k), lambda: fill.get(("kg", k))[:ML_W, :] * (ML_DH ** -0.5))
        fill.add(("mv", k), lambda: _dot(hb(), wcols(OFF_MV, ML_W)))

    def queue_body(k, with_gates):
        hb = lambda: fill.get(("hb", k))

        def proj(lo, width):
            return _dot(hb(), wcols(lo, width))

        def gmlp_inputs():
            sv = proj(OFF_SV, SG_W)
            mu = jnp.mean(sv, axis=-1, keepdims=True)
            svc = sv - mu
            var = jnp.mean(svc * svc, axis=-1, keepdims=True)
            return svc * lax.rsqrt(var + EPS) * lng_ref[...] + lnb_ref[...]

        def rope_tables():
            cos, sin = cos_ref[toks(k), :], sin_ref[toks(k), :]
            if R > 1:
                cos = jnp.concatenate([cos] * R, axis=0)
                sin = jnp.concatenate([sin] * R, axis=0)
            return cos, sin

        if not with_gates:
            fill.add(("mo", k), lambda: _sigmoid(proj(OFF_MO, ML_W)))
            fill.add(("su", k), lambda: proj(OFF_SU, SG_W))
            fill.add(("vnorm", k), gmlp_inputs)
            fill.add(("wq", k), lambda: (_rope(proj(OFF_WQ, SW_W), *rope_tables()) * (SW_DH ** -0.5)).astype(BF16))
            fill.add(("wk", k), lambda: _rope(proj(OFF_WK, SW_KVW), *rope_tables()))
            fill.add(("wv", k), lambda: proj(OFF_WV, SW_KVW))
        else:
            for i in range(3):
                fill.add(("gl", k, i), functools.partial(
                    lambda i: _sigmoid(proj(OFF_GL + i * D_MODEL, D_MODEL)), i))

    def queue_tail(k):
        rows = slice(k * T, (k + 1) * T)
        gate = lambda b: fill.get(("gl", k, b))
        fill.add(("sw", k), lambda: _dot(ysw_ref[rows, :], wbs_ref[...]))

        def out_proj():
            merged = gate(0) * fill.get(("ml", k))
            merged = merged + gate(1) * fill.get(("sg", k))
            merged = merged + gate(2) * fill.get(("sw", k))
            return _dot(merged.astype(BF16), wout_ref[...])

        def emit():
            y = load_x(k) + _rms(fill.get(("o", k))) * (_rows(g1_ref[...], R, Ls) * gpost_ref[...])
            xo_ref[:, toks(k), :] = y.reshape(R, Ls, D_MODEL)
            return None

        fill.add(("o", k), out_proj)
        fill.add(("emit", k), emit)

    ri = lax.broadcasted_iota(jnp.int32, (Ls, Ls), 0)
    ci = lax.broadcasted_iota(jnp.int32, (Ls, Ls), 1)
    triu = jnp.where(ri <= ci, 1.0, 0.0).astype(BF16)
    triu3 = jnp.concatenate([triu] * 3, axis=0)
    gi = lax.broadcasted_iota(jnp.int32, (Lg, Lg), 0)
    gj = lax.broadcasted_iota(jnp.int32, (Lg, Lg), 1)
    n_q = SW_REP * SW_CHUNK
    rid = lax.broadcasted_iota(jnp.int32, (n_q, 1), 0) // SW_CHUNK
    sink_cols = []
    for g in range(SW_KV):
        col = jnp.full((n_q, 1), sinks_ref[SW_REP * g + SW_REP - 1], F32)
        for i in range(SW_REP - 2, -1, -1):
            col = jnp.where(rid == i, sinks_ref[SW_REP * g + i], col)
        sink_cols.append(col)
    kpos_chunk = lax.broadcasted_iota(jnp.int32, (n_q, BAND), 1) // SW_CHUNK

    def mixers(k):
        rows = slice(k * T, (k + 1) * T)
        grow = fill.get(("grow", k))
        lf_row = jax.nn.log_sigmoid(grow)
        mqb, mkt, mv = fill.get(("mq", k)), fill.get(("mkt", k)), fill.get(("mv", k))
        for r in range(R):
            rs = slice(r * Ls, (r + 1) * Ls)
            brow_all = _dot(jnp.concatenate(_split3(lf_row[:, rs]), axis=1), triu3)
            bcol_all = brow_all.T
            for hd in range(ML_HEADS):
                hs = slice(hd * ML_DH, (hd + 1) * ML_DH)
                fg = ML_HEADS + hd
                fill.pump(w_ml)
                h_ml, caug_new, m_new = _mlstm_unit(
                    mqb[rs, hs], mkt[hs, rs], mv[rs, hs],
                    grow[hd:hd + 1, rs] - brow_all[fg:fg + 1, :], bcol_all[:, fg:fg + 1],
                    st_ref[r, hd], m_ref[r, hd:hd + 1, 0:1])
                st_ref[r, hd] = caug_new
                m_ref[r, hd:hd + 1, :] = jnp.broadcast_to(m_new, (1, LANES))
                yml_ref[k * T + r * Ls:k * T + (r + 1) * Ls, hs] = (
                    fill.get(("mo", k))[rs, hs] * h_ml).astype(BF16)
        fill.add(("ml", k), lambda: _dot(yml_ref[rows, :], wbm_ref[...]))

        vnorm = fill.get(("vnorm", k))
        su = fill.get(("su", k))
        if has_state:
            gv_ref[...] = vnorm.reshape(R, Ls, SG_W)
        n_chunk = T // Lg
        for g in range(SG_GROUPS):
            cs = slice(g * SG_GDIM, (g + 1) * SG_GDIM)
            fill.pump(w_sg)
            w = jnp.where(gi >= gj, wsp_ref[g, :Lg, :Lg], 0.0).astype(BF16)
            vcat = jnp.concatenate([vnorm[c * Lg:(c + 1) * Lg, cs] for c in range(n_chunk)], axis=1)
            z = _dot(w, vcat.astype(BF16))
            for c in range(n_chunk):
                ts = slice(c * Lg, (c + 1) * Lg)
                zc = z[:, c * SG_GDIM:(c + 1) * SG_GDIM] + bsp_ref[g, :Lg, :]
                ysg_ref[k * T + c * Lg:k * T + (c + 1) * Lg, cs] = (su[ts, cs] * zc).astype(BF16)
        fill.add(("sg", k), lambda: _dot(ysg_ref[rows, :], wbg_ref[...]))

        wqb, wk, wv = fill.get(("wq", k)), fill.get(("wk", k)), fill.get(("wv", k))
        for r in range(R):
            if has_state:
                kband = jnp.concatenate([ck_ref[r], wk[r * Ls:(r + 1) * Ls]], axis=0)
                vband = jnp.concatenate([cv_ref[r], wv[r * Ls:(r + 1) * Ls]], axis=0)
            else:
                kband = jnp.concatenate([kprev_ref[...], wk], axis=0)
                vband = jnp.concatenate([vprev_ref[...], wv], axis=0)
            lane = lax.broadcasted_iota(jnp.int32, kband.shape, 1)
            k_group = (lane // SW_HALF) % SW_KV
            v_group = lane // SW_DH
            kg = [jnp.where(k_group == g, kband, 0.0).astype(BF16) for g in range(SW_KV)]
            vaug = [jnp.concatenate([jnp.where(v_group == g, vband, 0.0),
                                     jnp.where(v_group == g, 1.0, 0.0)], axis=1).astype(BF16)
                    for g in range(SW_KV)]
            for j in range(n_sw):
                t0 = r * Ls + j * SW_CHUNK
                fill.pump(w_sw)
                qstack = jnp.concatenate([wqb[t0:t0 + SW_CHUNK, i * LANES:(i + 1) * LANES]
                                          for i in range(SW_REP)], axis=0)
                needs_mask = not has_state and k == 0 and j < WINDOW // SW_CHUNK
                mask = kpos_chunk >= (WINDOW // SW_CHUNK) - (seq * n_sub * n_sw + j) if needs_mask else None
                ks = slice(j * SW_CHUNK, j * SW_CHUNK + BAND)
                y = _swa_chunk(qstack, [a[ks] for a in kg], [a[ks] for a in vaug], sink_cols, mask)
                for i in range(SW_REP):
                    ysw_ref[k * T + t0:k * T + t0 + SW_CHUNK, i * LANES:(i + 1) * LANES] = (
                        y[i * SW_CHUNK:(i + 1) * SW_CHUNK].astype(BF16))
        if has_state:
            kn_ref[...] = wk.reshape(R, Ls, SW_KVW)
            vn_ref[...] = wv.reshape(R, Ls, SW_KVW)
        else:
            kprev_ref[...] = wk[T - WINDOW:]
            vprev_ref[...] = wv[T - WINDOW:]

    fill.add(("hb", 0), lambda: norm(0))
    queue_head(0)
    for k in range(n_sub):
        for name in ("grow", "mq", "mkt", "mv"):
            fill.get((name, k))
        queue_body(k, with_gates=False)
        if k + 1 < n_sub:
            fill.add(("hb", k + 1), functools.partial(norm, k + 1))
            fill.get(("hb", k + 1))
            queue_head(k + 1)
        queue_body(k, with_gates=True)
        mixers(k)
        queue_tail(k)
    fill.drain()

    @pl.when(seq == n_seq - 1)
    def _emit_state():
        if not has_state:
            kn_ref[0] = kprev_ref[...]
            vn_ref[0] = vprev_ref[...]
        for r in range(R):
            for hd in range(ML_HEADS):
                cout_ref[r, hd] = st_ref[r, hd, :, :ML_DH]
                nout_ref[r, hd:hd + 1, :] = st_ref[r, hd, :, ML_DH:].T[0:1, :]


def _const_spec(shape, layer):
    nd = len(shape)
    return pl.BlockSpec((None,) + tuple(shape), lambda b, s: (layer,) + (0,) * nd,
                        pipeline_mode=pl.Buffered(1))


def _mixer(x, mod, layer, wts, cos_t, sin_t, state, prev, *, R, Lr, Lg, n_sub):
    B, S, _ = x.shape
    n_seq = S // Lr
    has_state = state is not None
    T = R * Lr
    assert n_sub == 1 or R == 1

    def mod_spec(j):
        return pl.BlockSpec((None, None, R, 1, D_MODEL), lambda b, s: (layer, j, b, 0, 0))

    in_specs = [
        pl.BlockSpec((R, Lr, D_MODEL), lambda b, s: (b, s, 0)),
        mod_spec(0), mod_spec(1), mod_spec(2),
        *[_const_spec((D_MODEL, w), layer) for _, w in WIN_PIECES],
        _const_spec((ML_W + GATE_ROWS, D_MODEL), layer),
        _const_spec((1, D_MODEL), layer),
        _const_spec((1, D_MODEL), layer),
        _const_spec((GATE_ROWS, 1), layer),
        _const_spec((1, SG_W), layer),
        _const_spec((1, SG_W), layer),
        _const_spec((SG_GROUPS, SG_CHUNK, SG_CHUNK), layer),
        _const_spec((SG_GROUPS, SG_CHUNK, SG_GDIM), layer),
        pl.BlockSpec(memory_space=pltpu.SMEM),
        _const_spec((ML_W, D_MODEL), layer),
        _const_spec((SG_W, D_MODEL), layer),
        _const_spec((SW_W, D_MODEL), layer),
        _const_spec((D_MODEL, D_MODEL), layer),
        pl.BlockSpec((Lr, LANES), (lambda b, s: (0, 0)) if has_state else (lambda b, s: (s, 0))),
        pl.BlockSpec((Lr, LANES), (lambda b, s: (0, 0)) if has_state else (lambda b, s: (s, 0))),
    ]
    args = [x, mod, mod, mod, *wts["win"], wts["wkt"], wts["g_pre_mix"], wts["g_post_mix"],
            wts["bg_row"], wts["ln_v_g"], wts["ln_v_b"], wts["w_spatial"], wts["b_spatial_b"],
            wts["swa_sinks"][layer], wts["w_br_mlstm"], wts["w_br_gmlp"], wts["w_br_swa"], wts["w_out"],
            cos_t, sin_t]
    n_new = Lr if has_state else WINDOW
    stacked = [
        ((DEPTH, B, ML_HEADS, ML_DH, ML_DH), (None, R, ML_HEADS, ML_DH, ML_DH), lambda b, s: (layer, b, 0, 0, 0)),
        ((DEPTH, B, ML_HEADS, ML_DH), (None, R, ML_HEADS, ML_DH), lambda b, s: (layer, b, 0, 0)),
        ((DEPTH, B, n_new, SW_KVW), (None, R, n_new, SW_KVW), lambda b, s: (layer, b, 0, 0)),
        ((DEPTH, B, n_new, SW_KVW), (None, R, n_new, SW_KVW), lambda b, s: (layer, b, 0, 0)),
    ]
    if has_state:
        stacked.append(((DEPTH, B, S, SG_W), (None, R, Lr, SG_W), lambda b, s: (layer, b, s, 0)))
    st_shapes = [jax.ShapeDtypeStruct(shp, F32) for shp, _, _ in stacked]
    st_specs = [pl.BlockSpec(blk, imap) for _, blk, imap in stacked]
    x_out = (jax.ShapeDtypeStruct((B, S, D_MODEL), F32), pl.BlockSpec((R, Lr, D_MODEL), lambda b, s: (b, s, 0)))
    m_out = (jax.ShapeDtypeStruct((B, ML_HEADS, LANES), F32),
             pl.BlockSpec((R, ML_HEADS, LANES), lambda b, s: (b, 0, 0)))
    out_shape = [x_out[0], st_shapes[0], st_shapes[1], m_out[0]] + st_shapes[2:]
    out_specs = [x_out[1], st_specs[0], st_specs[1], m_out[1]] + st_specs[2:]
    stacked_out_idx = [1, 2] + list(range(4, 4 + len(stacked) - 2))
    scratch = [pltpu.VMEM((R, ML_HEADS, ML_DH, 2 * ML_DH), F32),
               pltpu.VMEM((T, ML_W), BF16), pltpu.VMEM((T, SG_W), BF16), pltpu.VMEM((T, SW_W), BF16)]
    if has_state:
        c_in, n_in, m_in, cache_k, cache_v = state
        in_specs += [
            pl.BlockSpec((None, R, ML_HEADS, ML_DH, ML_DH), lambda b, s: (layer, b, 0, 0, 0)),
            pl.BlockSpec((None, R, ML_HEADS, ML_DH), lambda b, s: (layer, b, 0, 0)),
            pl.BlockSpec((None, R, ML_HEADS, LANES), lambda b, s: (layer, b, 0, 0)),
            pl.BlockSpec((None, R, WINDOW, SW_KVW), lambda b, s: (layer, b, 0, 0)),
            pl.BlockSpec((None, R, WINDOW, SW_KVW), lambda b, s: (layer, b, 0, 0)),
        ]
        args += [c_in, n_in, m_in, cache_k, cache_v]
    else:
        scratch += [pltpu.VMEM((WINDOW, SW_KVW), F32), pltpu.VMEM((WINDOW, SW_KVW), F32)]
    if prev is None:
        prev = [jnp.zeros(sds.shape, sds.dtype) for sds in st_shapes]
    aliases = {}
    for arr, oi in zip(prev, stacked_out_idx):
        aliases[len(args)] = oi
        in_specs.append(pl.BlockSpec(memory_space=pl.ANY))
        args.append(arr)

    outs = pl.pallas_call(
        functools.partial(_mixer_kernel, R=R, Lr=Lr, Lg=Lg, has_state=has_state, n_seq=n_seq,
                          n_alias=len(aliases), n_sub=n_sub),
        grid=(B // R, n_seq),
        in_specs=in_specs,
        out_specs=out_specs,
        out_shape=out_shape,
        scratch_shapes=scratch,
        input_output_aliases=aliases,
        compiler_params=pltpu.CompilerParams(
            dimension_semantics=("parallel", "arbitrary"), vmem_limit_bytes=VMEM_LIMIT),
        name="mixer_sample" if has_state else "mixer_prompt",
    )(*args)
    return outs[0], outs[3], [outs[i] for i in stacked_out_idx]


def _ffn_kernel(x_ref, sh_ref, sc_ref, g_ref, gpre_ref, gpost_ref, wfi_ref, wfo_ref, o_ref, *, R, Lr, n_sub):
    Rs = R // n_sub if R > 1 else 1
    Ls = Lr if R > 1 else Lr // n_sub
    Ts = Rs * Ls

    def rows_of(ref, k):
        return _rows(ref[k * Rs:(k + 1) * Rs] if R > 1 else ref[...], Rs, Ls)

    def load(k):
        xk = x_ref[k * Rs:(k + 1) * Rs] if R > 1 else x_ref[:, k * Ls:(k + 1) * Ls, :]
        return xk.reshape(Ts, D_MODEL)

    def prologue(k):
        x = load(k)
        scale = gpre_ref[...] * (1.0 + rows_of(sc_ref, k))
        return (_rms(x) * scale + rows_of(sh_ref, k)).astype(BF16)

    def matmuls(hb):
        f = None
        for lo, hi in FFN_SPLITS:
            gate = _dot(hb, wfi_ref[:, lo:hi])
            up = _dot(hb, wfi_ref[:, FFN_HIDDEN + lo:FFN_HIDDEN + hi])
            act = (gate * (jnp.tanh(gate) + 1.0) * up).astype(BF16)
            part = _dot(act, wfo_ref[lo:hi, :])
            f = part if f is None else f + part
        return f

    def epilogue(k, f):
        y = (load(k) + _rms(f) * (rows_of(g_ref, k) * gpost_ref[...])).reshape(Rs, Ls, D_MODEL)
        if R > 1:
            o_ref[k * Rs:(k + 1) * Rs] = y
        else:
            o_ref[:, k * Ls:(k + 1) * Ls, :] = y

    hb = prologue(0)
    for k in range(n_sub):
        hb_next = prologue(k + 1) if k + 1 < n_sub else None
        f = matmuls(hb)
        epilogue(k, f)
        hb = hb_next


def _ffn(x, mod, layer, wts, *, R, Lr, n_sub):
    B, S, _ = x.shape

    def mod_spec(j):
        return pl.BlockSpec((None, None, R, 1, D_MODEL), lambda b, s: (layer, j, b, 0, 0))

    return pl.pallas_call(
        functools.partial(_ffn_kernel, R=R, Lr=Lr, n_sub=n_sub),
        grid=(B // R, S // Lr),
        in_specs=[
            pl.BlockSpec((R, Lr, D_MODEL), lambda b, s: (b, s, 0)),
            mod_spec(3), mod_spec(4), mod_spec(5),
            _const_spec((1, D_MODEL), layer),
            _const_spec((1, D_MODEL), layer),
            _const_spec((D_MODEL, 2 * FFN_HIDDEN), layer),
            _const_spec((FFN_HIDDEN, D_MODEL), layer),
        ],
        out_specs=pl.BlockSpec((R, Lr, D_MODEL), lambda b, s: (b, s, 0)),
        out_shape=jax.ShapeDtypeStruct((B, S, D_MODEL), F32),
        compiler_params=pltpu.CompilerParams(
            dimension_semantics=("parallel", "parallel"), vmem_limit_bytes=VMEM_LIMIT),
        name="ffn",
    )(x, mod, mod, mod, wts["g_pre_ffn"], wts["g_post_ffn"], wts["w_ffn_in"], wts["w_ffn_out"])


def _pair_layout(a):
    lead = a.shape[:-1]
    return jnp.swapaxes(a.reshape(lead + (2, 2, SW_HALF)), -3, -2).reshape(lead + (2 * SW_DH,))


def _query_layout(wq):
    lead = wq.shape[:-1]
    w = wq.reshape(lead + (SW_KV, SW_REP, 2, SW_HALF))
    return jnp.moveaxis(w, -4, -2).reshape(lead + (SW_W,))


def _rope_tables(pos):
    inv = ROPE_THETA ** (-jnp.arange(SW_HALF, dtype=F32) / SW_HALF)
    ang = pos.astype(F32)[:, None] * inv[None, :]
    cos, sin = jnp.cos(ang), jnp.sin(ang)
    cos_t = jnp.concatenate([cos, cos, cos, cos], axis=1)
    sin_t = jnp.concatenate([-sin, -sin, sin, sin], axis=1)
    return cos_t, sin_t


def _prep_weights(w_in, b_igate, b_fgate, g_pre_mix, g_post_mix, g_pre_ffn, g_post_ffn, ln_v_g, ln_v_b,
                  w_spatial, b_spatial, swa_sinks, w_br_mlstm, w_br_gmlp, w_br_swa, w_out, w_ffn_in, w_ffn_out):
    offs = np.cumsum((0,) + SPLIT_SIZES)
    seg = [w_in[..., offs[i]:offs[i + 1]] for i in range(len(SPLIT_SIZES))]
    mq, mk, mv, mo, mi, mf, su, sv, wq, wk, wv, gl = seg
    gates = jnp.concatenate([mi, mf], axis=-1)
    win = [a.astype(BF16) for a in (mq, mv, mo, su, sv, _query_layout(wq), _pair_layout(wk), wv, gl)]
    w_ffn_in = w_ffn_in * jnp.where(jnp.arange(2 * FFN_HIDDEN) < FFN_HIDDEN, 0.5, 1.0).astype(F32)
    bg = jnp.concatenate([b_igate, b_fgate], axis=-1)
    row = lambda a: a.reshape(DEPTH, 1, a.shape[-1])
    wbs = jnp.swapaxes(w_br_swa.reshape(DEPTH, SW_KV, SW_REP, SW_DH, D_MODEL), 1, 2).reshape(DEPTH, SW_W, D_MODEL)
    return dict(
        win=win,
        wkt=jnp.concatenate([jnp.swapaxes(mk, 1, 2),
                             jnp.pad(jnp.swapaxes(gates, 1, 2), ((0, 0), (0, GATE_ROWS - N_GATE), (0, 0)))],
                            axis=1).astype(BF16),
        bg_row=jnp.pad(bg, ((0, 0), (0, GATE_ROWS - N_GATE))).reshape(DEPTH, GATE_ROWS, 1),
        g_pre_mix=row(g_pre_mix), g_post_mix=row(g_post_mix),
        g_pre_ffn=row(g_pre_ffn), g_post_ffn=row(g_post_ffn),
        ln_v_g=row(ln_v_g), ln_v_b=row(ln_v_b),
        w_spatial=w_spatial,
        b_spatial_b=jnp.broadcast_to(b_spatial[..., None], b_spatial.shape + (SG_GDIM,)),
        swa_sinks=swa_sinks,
        w_br_mlstm=w_br_mlstm.astype(BF16), w_br_gmlp=w_br_gmlp.astype(BF16),
        w_br_swa=wbs.astype(BF16), w_out=w_out.astype(BF16),
        w_ffn_in=w_ffn_in.astype(BF16), w_ffn_out=w_ffn_out.astype(BF16),
    )


MIX_TOKENS_PROMPT = 512
MIX_SUB_PROMPT = 2
MIX_ROWS_SAMPLE = 4
FFN_TOKENS = 1024
FFN_SUB = 2


def kernel(x_prompt, x_sample, c_prompt, c_sample, state_mlstm_C, state_mlstm_n, state_mlstm_m, cache_swa_k, cache_swa_v, w_ada, b_ada, g_pre_mix, g_post_mix, g_pre_ffn, g_post_ffn, w_in, b_igate, b_fgate, ln_v_g, ln_v_b, w_spatial, b_spatial, swa_sinks, w_br_mlstm, w_br_gmlp, w_br_swa, w_out, w_ffn_in, w_ffn_out):
    Bp, Sp, _ = x_prompt.shape
    Bs, Ss, _ = x_sample.shape
    wts = _prep_weights(w_in, b_igate, b_fgate, g_pre_mix, g_post_mix, g_pre_ffn, g_post_ffn, ln_v_g, ln_v_b,
                        w_spatial, b_spatial, swa_sinks, w_br_mlstm, w_br_gmlp, w_br_swa, w_out,
                        w_ffn_in, w_ffn_out)
    mod = _modulation(jnp.concatenate([c_prompt, c_sample], axis=0), w_ada, b_ada)
    mod = mod.reshape(DEPTH, 6, Bp + Bs, 1, D_MODEL)
    mod_p, mod_s = mod[:, :, :Bp], mod[:, :, Bp:]
    cos_p, sin_p = _rope_tables(jnp.arange(Sp))
    cos_s, sin_s = _rope_tables(PAST_LEN + jnp.arange(Ss))

    m_in = jnp.broadcast_to(state_mlstm_m[..., None], state_mlstm_m.shape + (LANES,))
    ck = _pair_layout(cache_swa_k.reshape(DEPTH, Bs, WINDOW, SW_KVW))
    cv = cache_swa_v.reshape(DEPTH, Bs, WINDOW, SW_KVW)
    state = (state_mlstm_C, state_mlstm_n, m_in, ck, cv)

    xp, xs = x_prompt, x_sample
    st_p, st_s, ms_p, ms_s = None, None, [], []
    for l in range(DEPTH):
        xp, m_p, st_p = _mixer(xp, mod_p, l, wts, cos_p, sin_p, None, st_p,
                               R=1, Lr=MIX_TOKENS_PROMPT, Lg=SG_CHUNK, n_sub=MIX_SUB_PROMPT)
        xp = _ffn(xp, mod_p, l, wts, R=1, Lr=FFN_TOKENS, n_sub=FFN_SUB)
        xs, m_s, st_s = _mixer(xs, mod_s, l, wts, cos_s, sin_s, state, st_s,
                               R=MIX_ROWS_SAMPLE, Lr=Ss, Lg=Ss, n_sub=1)
        xs = _ffn(xs, mod_s, l, wts, R=FFN_TOKENS // Ss, Lr=Ss, n_sub=FFN_SUB)
        ms_p.append(m_p)
        ms_s.append(m_s)

    def unpack(st, ms, B, n_new):
        C, n, k, v = st[:4]
        m = jnp.stack(ms)[..., 0]
        k = _pair_layout(k).reshape(DEPTH, B, n_new, SW_KV, SW_DH)
        return C, n, m, k, v.reshape(DEPTH, B, n_new, SW_KV, SW_DH)

    C_p, n_p, m_p, k_p, v_p = unpack(st_p, ms_p, Bp, WINDOW)
    C_s, n_s, m_s, k_s, v_s = unpack(st_s, ms_s, Bs, Ss)
    return (xp, xs, C_p, n_p, m_p, k_p, v_p, C_s, n_s, m_s, k_s, v_s, st_s[4])
```

```python
import functools

import numpy as np
import jax
import jax.numpy as jnp
from jax import lax
from jax.experimental import pallas as pl
from jax.experimental.pallas import tpu as pltpu

F32 = jnp.float32
BF16 = jnp.bfloat16

D_MODEL = 1024
DEPTH = 4
PAST_LEN = 1024
ML_HEADS = 4
ML_DH = 128
ML_W = ML_HEADS * ML_DH
SG_GROUPS = 4
SG_CHUNK = 128
SG_W = 512
SG_GDIM = SG_W // SG_GROUPS
SW_HEADS = 8
SW_KV = 2
SW_REP = SW_HEADS // SW_KV
SW_DH = 64
SW_HALF = SW_DH // 2
SW_W = SW_HEADS * SW_DH
SW_KVW = SW_KV * SW_DH
SW_CHUNK = 64
WINDOW = 128
BAND = WINDOW + SW_CHUNK
ROPE_THETA = 10000.0
FFN_HIDDEN = 2816
EPS = 1e-6
SPLIT_SIZES = (ML_W, ML_W, ML_W, ML_W, ML_HEADS, ML_HEADS, SG_W, SG_W, SW_W, SW_KVW, SW_KVW, 3 * D_MODEL)

LANES = 128
N_GATE = 2 * ML_HEADS
GATE_ROWS = 16

OFF_MQ = 0
OFF_MV = OFF_MQ + ML_W
OFF_MO = OFF_MV + ML_W
OFF_SU = OFF_MO + ML_W
OFF_SV = OFF_SU + SG_W
OFF_WQ = OFF_SV + SG_W
OFF_WK = OFF_WQ + SW_W
OFF_WV = OFF_WK + SW_KVW
OFF_GL = OFF_WV + SW_KVW
D_IN_PAD = OFF_GL + 3 * D_MODEL

FFN_SPLITS = ((0, 1536), (1536, FFN_HIDDEN))

VMEM_LIMIT = 56 * 1024 * 1024

NT_DIMS = (((1,), (1,)), ((), ()))


def _dot(a, b):
    return jnp.dot(a, b, preferred_element_type=F32)


def _dot_nt(a, b):
    return lax.dot_general(a, b, NT_DIMS, preferred_element_type=F32)


def _sigmoid(x):
    return 0.5 * jnp.tanh(0.5 * x) + 0.5


def _rms(x):
    return x * lax.rsqrt(jnp.mean(x * x, axis=-1, keepdims=True) + EPS)


def _rows(m3, R, Lr):
    W = m3.shape[-1]
    if R == 1:
        return m3.reshape(1, W)
    return jnp.broadcast_to(m3, (R, Lr, W)).reshape(R * Lr, W)


def _split3(x):
    h1 = x.astype(BF16)
    r1 = x - h1.astype(F32)
    h2 = r1.astype(BF16)
    h3 = (r1 - h2.astype(F32)).astype(BF16)
    return h1, h2, h3


class _Fillers:
    def __init__(self, total_weight):
        self._thunks, self._vals, self._left = {}, {}, total_weight

    def add(self, name, thunk):
        self._thunks[name] = thunk

    def get(self, name):
        if name not in self._vals:
            self._vals[name] = self._thunks.pop(name)()
        return self._vals[name]

    def pump(self, weight):
        n = -(-len(self._thunks) * weight // max(self._left, 1))
        self._left -= weight
        for name in list(self._thunks)[:n]:
            self.get(name)

    def drain(self):
        for name in list(self._thunks):
            if name in self._thunks:
                self.get(name)


def _mod_kernel(c_ref, w_ref, b_ref, o_ref):
    c = c_ref[...]
    a = (c * _sigmoid(c)).astype(BF16)
    o_ref[...] = _dot(a, w_ref[...].astype(BF16)) + b_ref[...]


def _modulation(c_all, w_ada, b_ada):
    nb = c_all.shape[0]
    return pl.pallas_call(
        _mod_kernel,
        grid=(DEPTH, 6),
        in_specs=[
            pl.BlockSpec((nb, D_MODEL), lambda l, j: (0, 0)),
            pl.BlockSpec((None, D_MODEL, D_MODEL), lambda l, j: (l, 0, j)),
            pl.BlockSpec((None, None, 1, D_MODEL), lambda l, j: (l, j, 0, 0)),
        ],
        out_specs=pl.BlockSpec((None, None, nb, D_MODEL), lambda l, j: (l, j, 0, 0)),
        out_shape=jax.ShapeDtypeStruct((DEPTH, 6, nb, D_MODEL), F32),
        compiler_params=pltpu.CompilerParams(dimension_semantics=("parallel", "parallel")),
        name="adaln_modulation",
    )(c_all, w_ada, b_ada.reshape(DEPTH, 6, 1, D_MODEL))


def _rope(x, cos, sin_signed):
    T, W = x.shape
    outs = []
    for j in range(W // LANES):
        xb = x[:, j * LANES:(j + 1) * LANES]
        outs.append(xb * cos + pltpu.roll(xb, LANES // 2, 1) * sin_signed)
    return outs[0] if len(outs) == 1 else jnp.concatenate(outs, axis=1)


def _mlstm_unit(qb, kt, v, a_row, b_col, caug, m_prev):
    L = qb.shape[0]
    ti = lax.broadcasted_iota(jnp.int32, (L, L), 0)
    si = lax.broadcasted_iota(jnp.int32, (L, L), 1)
    a_tri = jnp.where(ti >= si, a_row, -jnp.inf)
    g = jnp.maximum(jnp.max(a_tri, axis=-1, keepdims=True), m_prev)
    w_intra = jnp.exp(a_tri - g)
    w_inter = jnp.exp(m_prev - g)
    s = _dot(qb, kt.astype(BF16)) * w_intra
    vaug = jnp.concatenate([v, jnp.ones_like(v)], axis=1).astype(BF16)
    qc = _dot(qb, caug.astype(BF16))
    sv = _dot(s.astype(BF16), vaug)
    num = w_inter * qc[:, :ML_DH] + sv[:, :ML_DH]
    den = w_inter * qc[:, ML_DH:] + sv[:, ML_DH:]
    floor = jnp.exp(-(jnp.broadcast_to(b_col, (L, ML_DH)) + g))
    h = num / jnp.maximum(jnp.abs(den), floor)
    g_last = g[L - 1:L, :]
    ws_row = jnp.exp(a_row - g_last)
    upd = _dot((kt * ws_row).astype(BF16), vaug)
    caug_new = jnp.exp(m_prev - g_last) * caug + upd
    return h, caug_new, b_col[L - 1:L, :] + g_last


def _swa_chunk(qstack, kg, vaug, sink_cols, mask):
    o = None
    e_cols = []
    for g in range(SW_KV):
        s = _dot_nt(qstack, kg[g])
        if mask is not None:
            s = jnp.where(mask, s, -jnp.inf)
        mx = jnp.maximum(jnp.max(s, axis=-1, keepdims=True), sink_cols[g])
        p = jnp.exp(s - mx).astype(BF16)
        e_cols.append(jnp.exp(sink_cols[g] - mx))
        og = _dot(p, vaug[g])
        o = og if o is None else o + og
    lane = lax.broadcasted_iota(jnp.int32, (qstack.shape[0], LANES), 1)
    e = jnp.where(lane < SW_DH, e_cols[0], e_cols[1])
    return o[:, :LANES] / (o[:, LANES:] + e)


def _mixer_kernel(*refs, R, Lr, Lg, has_state, n_seq, n_alias, n_sub):
    it = iter(refs)
    x_ref, sh1_ref, sc1_ref, g1_ref = next(it), next(it), next(it), next(it)
    win_ref, wkt_ref, gpre_ref, gpost_ref = next(it), next(it), next(it), next(it)
    bgr_ref = next(it)
    lng_ref, lnb_ref, wsp_ref, bsp_ref = next(it), next(it), next(it), next(it)
    sinks_ref = next(it)
    wbm_ref, wbg_ref, wbs_ref, wout_ref = next(it), next(it), next(it), next(it)
    cos_ref, sin_ref = next(it), next(it)
    if has_state:
        cin_ref, nin_ref, min_ref, ck_ref, cv_ref = next(it), next(it), next(it), next(it), next(it)
    for _ in range(n_alias):
        next(it)
    xo_ref, cout_ref, nout_ref, m_ref, kn_ref, vn_ref = (next(it), next(it), next(it), next(it), next(it),
                                                         next(it))
    if has_state:
        gv_ref = next(it)
    st_ref, yml_ref, ysg_ref, ysw_ref = next(it), next(it), next(it), next(it)
    if not has_state:
        kprev_ref, vprev_ref = next(it), next(it)

    Ls = Lr // n_sub
    T = R * Ls
    seq = pl.program_id(1)

    @pl.when(seq == 0)
    def _init():
        if has_state:
            for r in range(R):
                for hd in range(ML_HEADS):
                    st_ref[r, hd, :, :ML_DH] = cin_ref[r, hd]
                    st_ref[r, hd, :, ML_DH:] = jnp.broadcast_to(nin_ref[r, hd:hd + 1, :], (ML_DH, ML_DH)).T
            m_ref[...] = min_ref[...]
        else:
            st_ref[...] = jnp.zeros_like(st_ref)
            m_ref[...] = jnp.zeros_like(m_ref)
            kprev_ref[...] = jnp.zeros_like(kprev_ref)
            vprev_ref[...] = jnp.zeros_like(vprev_ref)

    n_sw = Ls // SW_CHUNK
    w_ml, w_sg, w_sw = (3, 1, 3) if Ls > SW_CHUNK else (1, 1, 1)
    fill = _Fillers(n_sub * (R * ML_HEADS * w_ml + SG_GROUPS * w_sg + R * n_sw * w_sw))

    def toks(k):
        return slice(k * Ls, (k + 1) * Ls)

    def load_x(k):
        return x_ref[:, toks(k), :].reshape(T, D_MODEL)

    def norm(k):
        scale = gpre_ref[...] * (1.0 + _rows(sc1_ref[...], R, Ls))
        return (_rms(load_x(k)) * scale + _rows(sh1_ref[...], R, Ls)).astype(BF16)

    def queue_head(k):
        hb = lambda: fill.get(("hb", k))
        fill.add(("kg", k), lambda: _dot_nt(wkt_ref[...], hb()))
        fill.add(("grow", k), lambda: fill.get(("kg", k))[ML_W:, :] + bgr_ref[...])
        fill.add(("mq", k), lambda: _dot(hb(), win_ref[:, OFF_MQ:OFF_MQ + ML_W]).astype(BF16))
        fill.add(("mkt", k), lambda: fill.get(("kg", k))[:ML_W, :] * (ML_DH ** -0.5))
        fill.add(("mv", k), lambda: _dot(hb(), win_ref[:, OFF_MV:OFF_MV + ML_W]))

    def queue_body(k, with_gates):
        hb = lambda: fill.get(("hb", k))

        def proj(lo, width):
            return _dot(hb(), win_ref[:, lo:lo + width])

        def gmlp_inputs():
            sv = proj(OFF_SV, SG_W)
            mu = jnp.mean(sv, axis=-1, keepdims=True)
            svc = sv - mu
            var = jnp.mean(svc * svc, axis=-1, keepdims=True)
            return svc * lax.rsqrt(var + EPS) * lng_ref[...] + lnb_ref[...]

        def rope_tables():
            cos, sin = cos_ref[toks(k), :], sin_ref[toks(k), :]
            if R > 1:
                cos = jnp.concatenate([cos] * R, axis=0)
                sin = jnp.concatenate([sin] * R, axis=0)
            return cos, sin

        if not with_gates:
            fill.add(("mo", k), lambda: _sigmoid(proj(OFF_MO, ML_W)))
            fill.add(("su", k), lambda: proj(OFF_SU, SG_W))
            fill.add(("vnorm", k), gmlp_inputs)
            fill.add(("wq", k), lambda: (_rope(proj(OFF_WQ, SW_W), *rope_tables()) * (SW_DH ** -0.5)).astype(BF16))
            fill.add(("wk", k), lambda: _rope(proj(OFF_WK, SW_KVW), *rope_tables()))
            fill.add(("wv", k), lambda: proj(OFF_WV, SW_KVW))
        else:
            for i in range(3):
                fill.add(("gl", k, i), functools.partial(
                    lambda i: _sigmoid(proj(OFF_GL + i * D_MODEL, D_MODEL)), i))

    def queue_tail(k):
        rows = slice(k * T, (k + 1) * T)
        gate = lambda b: fill.get(("gl", k, b))
        fill.add(("sw", k), lambda: _dot(ysw_ref[rows, :], wbs_ref[...]))

        def out_proj():
            merged = gate(0) * fill.get(("ml", k))
            merged = merged + gate(1) * fill.get(("sg", k))
            merged = merged + gate(2) * fill.get(("sw", k))
            return _dot(merged.astype(BF16), wout_ref[...])

        def emit():
            y = load_x(k) + _rms(fill.get(("o", k))) * (_rows(g1_ref[...], R, Ls) * gpost_ref[...])
            xo_ref[:, toks(k), :] = y.reshape(R, Ls, D_MODEL)
            return None

        fill.add(("o", k), out_proj)
        fill.add(("emit", k), emit)

    ri = lax.broadcasted_iota(jnp.int32, (Ls, Ls), 0)
    ci = lax.broadcasted_iota(jnp.int32, (Ls, Ls), 1)
    triu = jnp.where(ri <= ci, 1.0, 0.0).astype(BF16)
    triu3 = jnp.concatenate([triu] * 3, axis=0)
    gi = lax.broadcasted_iota(jnp.int32, (Lg, Lg), 0)
    gj = lax.broadcasted_iota(jnp.int32, (Lg, Lg), 1)
    n_q = SW_REP * SW_CHUNK
    rid = lax.broadcasted_iota(jnp.int32, (n_q, 1), 0) // SW_CHUNK
    sink_cols = []
    for g in range(SW_KV):
        col = jnp.full((n_q, 1), sinks_ref[SW_REP * g + SW_REP - 1], F32)
        for i in range(SW_REP - 2, -1, -1):
            col = jnp.where(rid == i, sinks_ref[SW_REP * g + i], col)
        sink_cols.append(col)
    kpos_chunk = lax.broadcasted_iota(jnp.int32, (n_q, BAND), 1) // SW_CHUNK

    def mixers(k):
        rows = slice(k * T, (k + 1) * T)
        grow = fill.get(("grow", k))
        lf_row = jax.nn.log_sigmoid(grow)
        mqb, mkt, mv = fill.get(("mq", k)), fill.get(("mkt", k)), fill.get(("mv", k))
        for r in range(R):
            rs = slice(r * Ls, (r + 1) * Ls)
            brow_all = _dot(jnp.concatenate(_split3(lf_row[:, rs]), axis=1), triu3)
            bcol_all = brow_all.T
            for hd in range(ML_HEADS):
                hs = slice(hd * ML_DH, (hd + 1) * ML_DH)
                fg = ML_HEADS + hd
                fill.pump(w_ml)
                h_ml, caug_new, m_new = _mlstm_unit(
                    mqb[rs, hs], mkt[hs, rs], mv[rs, hs],
                    grow[hd:hd + 1, rs] - brow_all[fg:fg + 1, :], bcol_all[:, fg:fg + 1],
                    st_ref[r, hd], m_ref[r, hd:hd + 1, 0:1])
                st_ref[r, hd] = caug_new
                m_ref[r, hd:hd + 1, :] = jnp.broadcast_to(m_new, (1, LANES))
                yml_ref[k * T + r * Ls:k * T + (r + 1) * Ls, hs] = (
                    fill.get(("mo", k))[rs, hs] * h_ml).astype(BF16)
        fill.add(("ml", k), lambda: _dot(yml_ref[rows, :], wbm_ref[...]))

        vnorm = fill.get(("vnorm", k))
        su = fill.get(("su", k))
        if has_state:
            gv_ref[...] = vnorm.reshape(R, Ls, SG_W)
        n_chunk = T // Lg
        for g in range(SG_GROUPS):
            cs = slice(g * SG_GDIM, (g + 1) * SG_GDIM)
            fill.pump(w_sg)
            w = jnp.where(gi >= gj, wsp_ref[g, :Lg, :Lg], 0.0).astype(BF16)
            vcat = jnp.concatenate([vnorm[c * Lg:(c + 1) * Lg, cs] for c in range(n_chunk)], axis=1)
            z = _dot(w, vcat.astype(BF16))
            for c in range(n_chunk):
                ts = slice(c * Lg, (c + 1) * Lg)
                zc = z[:, c * SG_GDIM:(c + 1) * SG_GDIM] + bsp_ref[g, :Lg, :]
                ysg_ref[k * T + c * Lg:k * T + (c + 1) * Lg, cs] = (su[ts, cs] * zc).astype(BF16)
        fill.add(("sg", k), lambda: _dot(ysg_ref[rows, :], wbg_ref[...]))

        wqb, wk, wv = fill.get(("wq", k)), fill.get(("wk", k)), fill.get(("wv", k))
        for r in range(R):
            if has_state:
                kband = jnp.concatenate([ck_ref[r], wk[r * Ls:(r + 1) * Ls]], axis=0)
                vband = jnp.concatenate([cv_ref[r], wv[r * Ls:(r + 1) * Ls]], axis=0)
            else:
                kband = jnp.concatenate([kprev_ref[...], wk], axis=0)
                vband = jnp.concatenate([vprev_ref[...], wv], axis=0)
            lane = lax.broadcasted_iota(jnp.int32, kband.shape, 1)
            k_group = (lane // SW_HALF) % SW_KV
            v_group = lane // SW_DH
            kg = [jnp.where(k_group == g, kband, 0.0).astype(BF16) for g in range(SW_KV)]
            vaug = [jnp.concatenate([jnp.where(v_group == g, vband, 0.0),
                                     jnp.where(v_group == g, 1.0, 0.0)], axis=1).astype(BF16)
                    for g in range(SW_KV)]
            for j in range(n_sw):
                t0 = r * Ls + j * SW_CHUNK
                fill.pump(w_sw)
                qstack = jnp.concatenate([wqb[t0:t0 + SW_CHUNK, i * LANES:(i + 1) * LANES]
                                          for i in range(SW_REP)], axis=0)
                needs_mask = not has_state and k == 0 and j < WINDOW // SW_CHUNK
                mask = kpos_chunk >= (WINDOW // SW_CHUNK) - (seq * n_sub * n_sw + j) if needs_mask else None
                ks = slice(j * SW_CHUNK, j * SW_CHUNK + BAND)
                y = _swa_chunk(qstack, [a[ks] for a in kg], [a[ks] for a in vaug], sink_cols, mask)
                for i in range(SW_REP):
                    ysw_ref[k * T + t0:k * T + t0 + SW_CHUNK, i * LANES:(i + 1) * LANES] = (
                        y[i * SW_CHUNK:(i + 1) * SW_CHUNK].astype(BF16))
        if has_state:
            kn_ref[...] = wk.reshape(R, Ls, SW_KVW)
            vn_ref[...] = wv.reshape(R, Ls, SW_KVW)
        else:
            kprev_ref[...] = wk[T - WINDOW:]
            vprev_ref[...] = wv[T - WINDOW:]

    fill.add(("hb", 0), lambda: norm(0))
    queue_head(0)
    for k in range(n_sub):
        for name in ("grow", "mq", "mkt", "mv"):
            fill.get((name, k))
        queue_body(k, with_gates=False)
        if k + 1 < n_sub:
            fill.add(("hb", k + 1), functools.partial(norm, k + 1))
            fill.get(("hb", k + 1))
            queue_head(k + 1)
        queue_body(k, with_gates=True)
        mixers(k)
        queue_tail(k)
    fill.drain()

    @pl.when(seq == n_seq - 1)
    def _emit_state():
        if not has_state:
            kn_ref[0] = kprev_ref[...]
            vn_ref[0] = vprev_ref[...]
        for r in range(R):
            for hd in range(ML_HEADS):
                cout_ref[r, hd] = st_ref[r, hd, :, :ML_DH]
                nout_ref[r, hd:hd + 1, :] = st_ref[r, hd, :, ML_DH:].T[0:1, :]


def _const_spec(shape, layer):
    nd = len(shape)
    return pl.BlockSpec((None,) + tuple(shape), lambda b, s: (layer,) + (0,) * nd,
                        pipeline_mode=pl.Buffered(1))


def _mixer(x, mod, layer, wts, cos_t, sin_t, state, prev, *, R, Lr, Lg, n_sub):
    B, S, _ = x.shape
    n_seq = S // Lr
    has_state = state is not None
    T = R * Lr
    assert n_sub == 1 or R == 1

    def mod_spec(j):
        return pl.BlockSpec((None, None, R, 1, D_MODEL), lambda b, s: (layer, j, b, 0, 0))

    in_specs = [
        pl.BlockSpec((R, Lr, D_MODEL), lambda b, s: (b, s, 0)),
        mod_spec(0), mod_spec(1), mod_spec(2),
        _const_spec((D_MODEL, D_IN_PAD), layer),
        _const_spec((ML_W + GATE_ROWS, D_MODEL), layer),
        _const_spec((1, D_MODEL), layer),
        _const_spec((1, D_MODEL), layer),
        _const_spec((GATE_ROWS, 1), layer),
        _const_spec((1, SG_W), layer),
        _const_spec((1, SG_W), layer),
        _const_spec((SG_GROUPS, SG_CHUNK, SG_CHUNK), layer),
        _const_spec((SG_GROUPS, SG_CHUNK, SG_GDIM), layer),
        pl.BlockSpec(memory_space=pltpu.SMEM),
        _const_spec((ML_W, D_MODEL), layer),
        _const_spec((SG_W, D_MODEL), layer),
        _const_spec((SW_W, D_MODEL), layer),
        _const_spec((D_MODEL, D_MODEL), layer),
        pl.BlockSpec((Lr, LANES), (lambda b, s: (0, 0)) if has_state else (lambda b, s: (s, 0))),
        pl.BlockSpec((Lr, LANES), (lambda b, s: (0, 0)) if has_state else (lambda b, s: (s, 0))),
    ]
    args = [x, mod, mod, mod, wts["win"], wts["wkt"], wts["g_pre_mix"], wts["g_post_mix"],
            wts["bg_row"], wts["ln_v_g"], wts["ln_v_b"], wts["w_spatial"], wts["b_spatial_b"],
            wts["swa_sinks"][layer], wts["w_br_mlstm"], wts["w_br_gmlp"], wts["w_br_swa"], wts["w_out"],
            cos_t, sin_t]
    n_new = Lr if has_state else WINDOW
    stacked = [
        ((DEPTH, B, ML_HEADS, ML_DH, ML_DH), (None, R, ML_HEADS, ML_DH, ML_DH), lambda b, s: (layer, b, 0, 0, 0)),
        ((DEPTH, B, ML_HEADS, ML_DH), (None, R, ML_HEADS, ML_DH), lambda b, s: (layer, b, 0, 0)),
        ((DEPTH, B, n_new, SW_KVW), (None, R, n_new, SW_KVW), lambda b, s: (layer, b, 0, 0)),
        ((DEPTH, B, n_new, SW_KVW), (None, R, n_new, SW_KVW), lambda b, s: (layer, b, 0, 0)),
    ]
    if has_state:
        stacked.append(((DEPTH, B, S, SG_W), (None, R, Lr, SG_W), lambda b, s: (layer, b, s, 0)))
    st_shapes = [jax.ShapeDtypeStruct(shp, F32) for shp, _, _ in stacked]
    st_specs = [pl.BlockSpec(blk, imap) for _, blk, imap in stacked]
    x_out = (jax.ShapeDtypeStruct((B, S, D_MODEL), F32), pl.BlockSpec((R, Lr, D_MODEL), lambda b, s: (b, s, 0)))
    m_out = (jax.ShapeDtypeStruct((B, ML_HEADS, LANES), F32),
             pl.BlockSpec((R, ML_HEADS, LANES), lambda b, s: (b, 0, 0)))
    out_shape = [x_out[0], st_shapes[0], st_shapes[1], m_out[0]] + st_shapes[2:]
    out_specs = [x_out[1], st_specs[0], st_specs[1], m_out[1]] + st_specs[2:]
    stacked_out_idx = [1, 2] + list(range(4, 4 + len(stacked) - 2))
    scratch = [pltpu.VMEM((R, ML_HEADS, ML_DH, 2 * ML_DH), F32),
               pltpu.VMEM((T, ML_W), BF16), pltpu.VMEM((T, SG_W), BF16), pltpu.VMEM((T, SW_W), BF16)]
    if has_state:
        c_in, n_in, m_in, cache_k, cache_v = state
        in_specs += [
            pl.BlockSpec((None, R, ML_HEADS, ML_DH, ML_DH), lambda b, s: (layer, b, 0, 0, 0)),
            pl.BlockSpec((None, R, ML_HEADS, ML_DH), lambda b, s: (layer, b, 0, 0)),
            pl.BlockSpec((None, R, ML_HEADS, LANES), lambda b, s: (layer, b, 0, 0)),
            pl.BlockSpec((None, R, WINDOW, SW_KVW), lambda b, s: (layer, b, 0, 0)),
            pl.BlockSpec((None, R, WINDOW, SW_KVW), lambda b, s: (layer, b, 0, 0)),
        ]
        args += [c_in, n_in, m_in, cache_k, cache_v]
    else:
        scratch += [pltpu.VMEM((WINDOW, SW_KVW), F32), pltpu.VMEM((WINDOW, SW_KVW), F32)]
    if prev is None:
        prev = [jnp.zeros(sds.shape, sds.dtype) for sds in st_shapes]
    aliases = {}
    for arr, oi in zip(prev, stacked_out_idx):
        aliases[len(args)] = oi
        in_specs.append(pl.BlockSpec(memory_space=pl.ANY))
        args.append(arr)

    outs = pl.pallas_call(
        functools.partial(_mixer_kernel, R=R, Lr=Lr, Lg=Lg, has_state=has_state, n_seq=n_seq,
                          n_alias=len(aliases), n_sub=n_sub),
        grid=(B // R, n_seq),
        in_specs=in_specs,
        out_specs=out_specs,
        out_shape=out_shape,
        scratch_shapes=scratch,
        input_output_aliases=aliases,
        compiler_params=pltpu.CompilerParams(
            dimension_semantics=("parallel", "arbitrary"), vmem_limit_bytes=VMEM_LIMIT),
        name="mixer_sample" if has_state else "mixer_prompt",
    )(*args)
    return outs[0], outs[3], [outs[i] for i in stacked_out_idx]


def _ffn_kernel(*refs, R, Lr, n_sub, ahead):
    x_ref, sh_ref, sc_ref, g_ref, gpre_ref, gpost_ref, wfi_ref, wfo_ref = refs[:8]
    if ahead:
        xn_ref, shn_ref, scn_ref, o_ref, hbn_ref = refs[8:]
    else:
        o_ref, = refs[8:]
    Rs = R // n_sub if R > 1 else 1
    Ls = Lr if R > 1 else Lr // n_sub
    Ts = Rs * Ls

    def rows_of(ref, k):
        return _rows(ref[k * Rs:(k + 1) * Rs] if R > 1 else ref[...], Rs, Ls)

    def load(k):
        xk = x_ref[k * Rs:(k + 1) * Rs] if R > 1 else x_ref[:, k * Ls:(k + 1) * Ls, :]
        return xk.reshape(Ts, D_MODEL)

    def prologue(k):
        x = load(k)
        scale = gpre_ref[...] * (1.0 + rows_of(sc_ref, k))
        return (_rms(x) * scale + rows_of(sh_ref, k)).astype(BF16)

    def matmuls(hb):
        f = None
        for lo, hi in FFN_SPLITS:
            gate = _dot(hb, wfi_ref[:, lo:hi])
            up = _dot(hb, wfi_ref[:, FFN_HIDDEN + lo:FFN_HIDDEN + hi])
            act = (gate * (jnp.tanh(gate) + 1.0) * up).astype(BF16)
            part = _dot(act, wfo_ref[lo:hi, :])
            f = part if f is None else f + part
        return f

    def epilogue(k, f):
        y = (load(k) + _rms(f) * (rows_of(g_ref, k) * gpost_ref[...])).reshape(Rs, Ls, D_MODEL)
        if R > 1:
            o_ref[k * Rs:(k + 1) * Rs] = y
        else:
            o_ref[:, k * Ls:(k + 1) * Ls, :] = y

    if ahead:
        @pl.when((pl.program_id(0) == 0) & (pl.program_id(1) == 0))
        def _first_norm():
            hbn_ref[...] = prologue(0)

        hb = hbn_ref[...]
    else:
        hb = prologue(0)
    for k in range(n_sub):
        hb_next = prologue(k + 1) if k + 1 < n_sub else None
        if ahead and k == 0:
            scale = gpre_ref[...] * (1.0 + _rows(scn_ref[...], Rs, Ls))
            hbn_ref[...] = (_rms(xn_ref[...].reshape(Ts, D_MODEL)) * scale + _rows(shn_ref[...], Rs, Ls)).astype(BF16)
        f = matmuls(hb)
        epilogue(k, f)
        hb = hb_next


def _ffn(x, mod, layer, wts, *, R, Lr, n_sub):
    B, S, _ = x.shape
    n_seq = S // Lr
    ahead = R == 1

    def mod_spec(j):
        return pl.BlockSpec((None, None, R, 1, D_MODEL), lambda b, s: (layer, j, b, 0, 0))

    in_specs = [
        pl.BlockSpec((R, Lr, D_MODEL), lambda b, s: (b, s, 0)),
        mod_spec(3), mod_spec(4), mod_spec(5),
        _const_spec((1, D_MODEL), layer),
        _const_spec((1, D_MODEL), layer),
        _const_spec((D_MODEL, 2 * FFN_HIDDEN), layer),
        _const_spec((FFN_HIDDEN, D_MODEL), layer),
    ]
    args = [x, mod, mod, mod, wts["g_pre_ffn"], wts["g_post_ffn"], wts["w_ffn_in"], wts["w_ffn_out"]]
    scratch = []
    if ahead:
        Ls = Lr // n_sub
        last = B * n_seq - 1
        nxt = lambda b, s: jnp.minimum(b * n_seq + s + 1, last)
        in_specs += [
            pl.BlockSpec((1, Ls, D_MODEL), lambda b, s: (nxt(b, s) // n_seq, (nxt(b, s) % n_seq) * n_sub, 0)),
            pl.BlockSpec((None, None, 1, 1, D_MODEL), lambda b, s: (layer, 3, nxt(b, s) // n_seq, 0, 0)),
            pl.BlockSpec((None, None, 1, 1, D_MODEL), lambda b, s: (layer, 4, nxt(b, s) // n_seq, 0, 0)),
        ]
        args += [x, mod, mod]
        scratch = [pltpu.VMEM((Ls, D_MODEL), BF16)]
    return pl.pallas_call(
        functools.partial(_ffn_kernel, R=R, Lr=Lr, n_sub=n_sub, ahead=ahead),
        grid=(B // R, n_seq),
        in_specs=in_specs,
        out_specs=pl.BlockSpec((R, Lr, D_MODEL), lambda b, s: (b, s, 0)),
        out_shape=jax.ShapeDtypeStruct((B, S, D_MODEL), F32),
        scratch_shapes=scratch,
        compiler_params=pltpu.CompilerParams(
            dimension_semantics=("arbitrary", "arbitrary") if ahead else ("parallel", "parallel"),
            vmem_limit_bytes=VMEM_LIMIT),
        name="ffn",
    )(*args)


def _pair_layout(a):
    lead = a.shape[:-1]
    return jnp.swapaxes(a.reshape(lead + (2, 2, SW_HALF)), -3, -2).reshape(lead + (2 * SW_DH,))


def _query_layout(wq):
    lead = wq.shape[:-1]
    w = wq.reshape(lead + (SW_KV, SW_REP, 2, SW_HALF))
    return jnp.moveaxis(w, -4, -2).reshape(lead + (SW_W,))


def _rope_tables(pos):
    inv = ROPE_THETA ** (-jnp.arange(SW_HALF, dtype=F32) / SW_HALF)
    ang = pos.astype(F32)[:, None] * inv[None, :]
    cos, sin = jnp.cos(ang), jnp.sin(ang)
    cos_t = jnp.concatenate([cos, cos, cos, cos], axis=1)
    sin_t = jnp.concatenate([-sin, -sin, sin, sin], axis=1)
    return cos_t, sin_t


def _prep_weights(w_in, b_igate, b_fgate, g_pre_mix, g_post_mix, g_pre_ffn, g_post_ffn, ln_v_g, ln_v_b,
                  w_spatial, b_spatial, swa_sinks, w_br_mlstm, w_br_gmlp, w_br_swa, w_out, w_ffn_in, w_ffn_out):
    offs = np.cumsum((0,) + SPLIT_SIZES)
    seg = [w_in[..., offs[i]:offs[i + 1]] for i in range(len(SPLIT_SIZES))]
    mq, mk, mv, mo, mi, mf, su, sv, wq, wk, wv, gl = seg
    gates = jnp.concatenate([mi, mf], axis=-1)
    win = jnp.concatenate([a.astype(BF16) for a in
                           (mq, mv, mo, su, sv, _query_layout(wq), _pair_layout(wk), wv, gl)], axis=-1)
    w_ffn_in = w_ffn_in * jnp.where(jnp.arange(2 * FFN_HIDDEN) < FFN_HIDDEN, 0.5, 1.0).astype(F32)
    bg = jnp.concatenate([b_igate, b_fgate], axis=-1)
    row = lambda a: a.reshape(DEPTH, 1, a.shape[-1])
    wbs = jnp.swapaxes(w_br_swa.reshape(DEPTH, SW_KV, SW_REP, SW_DH, D_MODEL), 1, 2).reshape(DEPTH, SW_W, D_MODEL)
    return dict(
        win=win,
        wkt=jnp.concatenate([jnp.swapaxes(mk, 1, 2),
                             jnp.pad(jnp.swapaxes(gates, 1, 2), ((0, 0), (0, GATE_ROWS - N_GATE), (0, 0)))],
                            axis=1).astype(BF16),
        bg_row=jnp.pad(bg, ((0, 0), (0, GATE_ROWS - N_GATE))).reshape(DEPTH, GATE_ROWS, 1),
        g_pre_mix=row(g_pre_mix), g_post_mix=row(g_post_mix),
        g_pre_ffn=row(g_pre_ffn), g_post_ffn=row(g_post_ffn),
        ln_v_g=row(ln_v_g), ln_v_b=row(ln_v_b),
        w_spatial=w_spatial,
        b_spatial_b=jnp.broadcast_to(b_spatial[..., None], b_spatial.shape + (SG_GDIM,)),
        swa_sinks=swa_sinks,
        w_br_mlstm=w_br_mlstm.astype(BF16), w_br_gmlp=w_br_gmlp.astype(BF16),
        w_br_swa=wbs.astype(BF16), w_out=w_out.astype(BF16),
        w_ffn_in=w_ffn_in.astype(BF16), w_ffn_out=w_ffn_out.astype(BF16),
    )


MIX_TOKENS_PROMPT = 512
MIX_SUB_PROMPT = 2
MIX_ROWS_SAMPLE = 4
FFN_TOKENS = 1024
FFN_SUB = 2


def kernel(x_prompt, x_sample, c_prompt, c_sample, state_mlstm_C, state_mlstm_n, state_mlstm_m, cache_swa_k, cache_swa_v, w_ada, b_ada, g_pre_mix, g_post_mix, g_pre_ffn, g_post_ffn, w_in, b_igate, b_fgate, ln_v_g, ln_v_b, w_spatial, b_spatial, swa_sinks, w_br_mlstm, w_br_gmlp, w_br_swa, w_out, w_ffn_in, w_ffn_out):
    Bp, Sp, _ = x_prompt.shape
    Bs, Ss, _ = x_sample.shape
    wts = _prep_weights(w_in, b_igate, b_fgate, g_pre_mix, g_post_mix, g_pre_ffn, g_post_ffn, ln_v_g, ln_v_b,
                        w_spatial, b_spatial, swa_sinks, w_br_mlstm, w_br_gmlp, w_br_swa, w_out,
                        w_ffn_in, w_ffn_out)
    mod = _modulation(jnp.concatenate([c_prompt, c_sample], axis=0), w_ada, b_ada)
    mod = mod.reshape(DEPTH, 6, Bp + Bs, 1, D_MODEL)
    mod_p, mod_s = mod[:, :, :Bp], mod[:, :, Bp:]
    cos_p, sin_p = _rope_tables(jnp.arange(Sp))
    cos_s, sin_s = _rope_tables(PAST_LEN + jnp.arange(Ss))

    m_in = jnp.broadcast_to(state_mlstm_m[..., None], state_mlstm_m.shape + (LANES,))
    ck = _pair_layout(cache_swa_k.reshape(DEPTH, Bs, WINDOW, SW_KVW))
    cv = cache_swa_v.reshape(DEPTH, Bs, WINDOW, SW_KVW)
    state = (state_mlstm_C, state_mlstm_n, m_in, ck, cv)

    xp, xs = x_prompt, x_sample
    st_p, st_s, ms_p, ms_s = None, None, [], []
    for l in range(DEPTH):
        xp, m_p, st_p = _mixer(xp, mod_p, l, wts, cos_p, sin_p, None, st_p,
                               R=1, Lr=MIX_TOKENS_PROMPT, Lg=SG_CHUNK, n_sub=MIX_SUB_PROMPT)
        xp = _ffn(xp, mod_p, l, wts, R=1, Lr=FFN_TOKENS, n_sub=FFN_SUB)
        xs, m_s, st_s = _mixer(xs, mod_s, l, wts, cos_s, sin_s, state, st_s,
                               R=MIX_ROWS_SAMPLE, Lr=Ss, Lg=Ss, n_sub=1)
        xs = _ffn(xs, mod_s, l, wts, R=FFN_TOKENS // Ss, Lr=Ss, n_sub=FFN_SUB)
        ms_p.append(m_p)
        ms_s.append(m_s)

    def unpack(st, ms, B, n_new):
        C, n, k, v = st[:4]
        m = jnp.stack(ms)[..., 0]
        k = _pair_layout(k).reshape(DEPTH, B, n_new, SW_KV, SW_DH)
        return C, n, m, k, v.reshape(DEPTH, B, n_new, SW_KV, SW_DH)

    C_p, n_p, m_p, k_p, v_p = unpack(st_p, ms_p, Bp, WINDOW)
    C_s, n_s, m_s, k_s, v_s = unpack(st_s, ms_s, Bs, Ss)
    return (xp, xs, C_p, n_p, m_p, k_p, v_p, C_s, n_s, m_s, k_s, v_s, st_s[4])
```
